```python
import jax, jax.numpy as jnp
from jax import lax
import numpy as np

D_MODEL = 2048
BATCH = 32
SEQ = 256
DEPTH = 2
DEC_BATCH = 2
DEC_SEQ = 2048
PAST_LEN = 256

GRID_W = 64
NA_HEADS = 16
NA_HEAD_DIM = 64
NA_WIDTH = NA_HEADS * NA_HEAD_DIM
WIN_H = 8
WIN_W = 16
QB_W = 16
KB_W = QB_W + WIN_W
CTX_QBLK = 128
POOL_WIDTH = D_MODEL // 2
POOL_WINDOWS = (2, 4, 8, 16)
POOL_GROUPS = len(POOL_WINDOWS)
POOL_GC = POOL_WIDTH // POOL_GROUPS
IN_WIDTH = 3 * NA_WIDTH + POOL_WIDTH + 2 * D_MODEL
N_EXPERTS = 16
N_GROUPS = 4
EXPERTS_PER_GROUP = N_EXPERTS // N_GROUPS
TOP_K = 2
EXPERT_FF = D_MODEL // 4
N_MOD = 6
EPS = 1e-6

kernel_name = "hybrid_natten_pool_moe_diffusion_step"


def rmsnorm(x, g):
    x32 = x.astype(jnp.float32)
    r = lax.rsqrt(jnp.mean(x32 * x32, axis=-1, keepdims=True) + EPS)
    return (x32 * r).astype(x.dtype) * g


def ada_params(cond, w, b):
    m = jax.nn.silu(cond) @ w + b
    return jnp.split(m, N_MOD, axis=-1)


def modulate(h, shift, scale):
    return h * (1 + scale) + shift


def split_in(p):
    offs = (NA_WIDTH, 2 * NA_WIDTH, 3 * NA_WIDTH, 3 * NA_WIDTH + POOL_WIDTH,
            3 * NA_WIDTH + POOL_WIDTH + D_MODEL)
    return jnp.split(p, offs, axis=-1)


def context_attention(q, k, v):
    B, L, H, dh = q.shape
    nb = L // CTX_QBLK
    scale = dh ** -0.5
    qb = q.reshape(B, nb, CTX_QBLK, H, dh).transpose(1, 0, 2, 3, 4)

    def block(qblk):
        s = jnp.einsum('bqhd,bkhd->bhqk', qblk, k, preferred_element_type=jnp.float32) * scale
        p = jax.nn.softmax(s, axis=-1)
        return jnp.einsum('bhqk,bkhd->bqhd', p.astype(v.dtype), v)

    o = lax.map(block, qb)
    return o.transpose(1, 0, 2, 3, 4).reshape(B, L, H * dh)


def neighbourhood_attention(q, k, v, ck, cv, rpb):
    B, N, H, dh = q.shape
    rows = N // GRID_W
    wr = min(WIN_H, rows)
    ncb = GRID_W // QB_W
    scale = dh ** -0.5
    r = jnp.arange(rows)
    rs = jnp.clip(r - WIN_H // 2, 0, rows - wr)
    krow = rs[:, None] + jnp.arange(wr)[None, :]
    cb_start = jnp.arange(ncb) * QB_W
    kcb = jnp.clip(cb_start - WIN_W // 2, 0, GRID_W - KB_W)
    kcol = kcb[:, None] + jnp.arange(KB_W)[None, :]
    qcol = cb_start[:, None] + jnp.arange(QB_W)[None, :]
    cstart = jnp.clip(qcol - WIN_W // 2, 0, GRID_W - WIN_W)
    valid = (kcol[:, None, :] >= cstart[:, :, None]) & (kcol[:, None, :] < cstart[:, :, None] + WIN_W)
    nloc = wr * KB_W
    valid = jnp.broadcast_to(valid[:, :, None, :], (ncb, QB_W, wr, KB_W)).reshape(ncb, QB_W, nloc)
    dr_idx = krow - r[:, None] + (WIN_H - 1)
    dc_idx = jnp.clip(kcol[:, None, :] - qcol[:, :, None] + (WIN_W - 1), 0, 2 * WIN_W - 2)
    bias = rpb[:, dr_idx[:, None, None, :, None], dc_idx[None, :, :, None, :]]
    bias = bias.reshape(H, rows, ncb, QB_W, nloc).astype(jnp.float32)

    def gather(t):
        tg = t.reshape(B, rows, GRID_W, H, dh)
        g = tg[:, krow[:, None, :, None], kcol[None, :, None, :]]
        return g.reshape(B, rows, ncb, nloc, H, dh)

    kg = gather(k)
    vg = gather(v)
    qg = q.reshape(B, rows, ncb, QB_W, H, dh)
    s_loc = jnp.einsum('brcqhd,brckhd->bhrcqk', qg, kg, preferred_element_type=jnp.float32) * scale + bias[None]
    s_loc = jnp.where(valid[None, None, None], s_loc, -jnp.inf)
    s_ctx = jnp.einsum('brcqhd,bkhd->bhrcqk', qg, ck, preferred_element_type=jnp.float32) * scale
    p = jax.nn.softmax(jnp.concatenate([s_loc, s_ctx], axis=-1), axis=-1)
    p_loc = p[..., :nloc].astype(v.dtype)
    p_ctx = p[..., nloc:].astype(v.dtype)
    o = (jnp.einsum('bhrcqk,brckhd->brcqhd', p_loc, vg)
         + jnp.einsum('bhrcqk,bkhd->brcqhd', p_ctx, cv))
    return o.reshape(B, N, H * dh)


def centred_window_mean(u, w):
    n = u.shape[-2]
    cs = jnp.cumsum(u.astype(jnp.float32), axis=-2)
    cs = jnp.pad(cs, [(0, 0)] * (u.ndim - 2) + [(1, 0), (0, 0)])
    t = jnp.arange(n)
    lo = jnp.clip(t - w // 2, 0, n)
    hi = jnp.clip(t + w - w // 2, 0, n)
    s = jnp.take(cs, hi, axis=-2) - jnp.take(cs, lo, axis=-2)
    cnt = (hi - lo).astype(jnp.float32)[:, None]
    return s / cnt


def pool_mixer(u, w_pool, pool_scale):
    outs = []
    for gi, w in enumerate(POOL_WINDOWS):
        ug = u[..., gi * POOL_GC:(gi + 1) * POOL_GC]
        outs.append((centred_window_mean(ug, w) - ug.astype(jnp.float32)).astype(u.dtype))
    d = jnp.stack(outs, axis=-2)
    y = jnp.einsum('...gc,gce->...ge', d, w_pool)
    return y.reshape(u.shape) * pool_scale


def merge_branches(o_a, o_p, ga, gp, w_ba, w_bp, w_out):
    a = o_a @ w_ba
    p = o_p @ w_bp
    return (jax.nn.sigmoid(ga) * a + jax.nn.sigmoid(gp) * p) @ w_out


def moe(h, w_router, router_bias, w_gate, w_up, w_down):
    shp = h.shape
    t = h.reshape(-1, shp[-1])
    scores = jax.nn.sigmoid(jnp.einsum('td,de->te', t, w_router, preferred_element_type=jnp.float32))
    sel = scores + router_bias.astype(jnp.float32)
    grp = sel.reshape(-1, N_GROUPS, EXPERTS_PER_GROUP)
    grp_score = lax.top_k(grp, TOP_K)[0].sum(-1)
    best = jnp.argmax(grp_score, axis=-1)
    in_grp = (jnp.arange(N_EXPERTS) // EXPERTS_PER_GROUP)[None, :] == best[:, None]
    masked = jnp.where(in_grp, sel, -jnp.inf)
    _, idx = lax.top_k(masked, TOP_K)
    wsel = jnp.take_along_axis(scores, idx, axis=-1)
    wsel = wsel / jnp.sum(wsel, axis=-1, keepdims=True)
    gates = jnp.sum(jax.nn.one_hot(idx, N_EXPERTS, dtype=jnp.float32) * wsel[..., None], axis=1)
    g = jnp.einsum('td,edf->tef', t, w_gate)
    u = jnp.einsum('td,edf->tef', t, w_up)
    act = jax.nn.silu(g) * u * gates[..., None].astype(t.dtype)
    y = jnp.einsum('tef,efd->td', act, w_down)
    return y.reshape(shp)


def setup_inputs(seed: int = 0) -> dict:
    key = jax.random.key(seed)
    ks = jax.random.split(key, 24)
    f32 = jnp.float32
    nrm = lambda k, shape, s: jax.random.normal(k, shape, f32) * s
    return {
        "x_prompt": nrm(ks[0], (BATCH, SEQ, D_MODEL), 1.0),
        "x_sample": nrm(ks[1], (DEC_BATCH, DEC_SEQ, D_MODEL), 1.0),
        "c": nrm(ks[2], (DEC_BATCH, D_MODEL), 1.0),
        "cache_k": nrm(ks[3], (DEC_BATCH, DEPTH, PAST_LEN, NA_HEADS, NA_HEAD_DIM), 1.0),
        "cache_v": nrm(ks[4], (DEC_BATCH, DEPTH, PAST_LEN, NA_HEADS, NA_HEAD_DIM), 1.0),
        "c_ctx": nrm(ks[5], (D_MODEL,), 1.0),
        "ada_w": nrm(ks[6], (DEPTH, D_MODEL, N_MOD * D_MODEL), 0.5 * D_MODEL ** -0.5),
        "ada_b": nrm(ks[7], (DEPTH, N_MOD * D_MODEL), 0.02),
        "norm1_g": 1.0 + nrm(ks[8], (DEPTH, D_MODEL), 0.02),
        "w_in": nrm(ks[9], (DEPTH, D_MODEL, IN_WIDTH), D_MODEL ** -0.5),
        "rpb": nrm(ks[10], (DEPTH, NA_HEADS, 2 * WIN_H - 1, 2 * WIN_W - 1), 0.2),
        "pool_w": nrm(ks[11], (DEPTH, POOL_GROUPS, POOL_GC, POOL_GC), POOL_GC ** -0.5),
        "pool_scale": 1.0 + nrm(ks[12], (DEPTH, POOL_WIDTH), 0.05),
        "w_branch_a": nrm(ks[13], (DEPTH, NA_WIDTH, D_MODEL), NA_WIDTH ** -0.5),
        "w_branch_p": nrm(ks[14], (DEPTH, POOL_WIDTH, D_MODEL), POOL_WIDTH ** -0.5),
        "w_out": nrm(ks[15], (DEPTH, D_MODEL, D_MODEL), D_MODEL ** -0.5),
        "norm2_g": 1.0 + nrm(ks[16], (DEPTH, D_MODEL), 0.02),
        "router_w": nrm(ks[17], (D_MODEL, N_EXPERTS), D_MODEL ** -0.5),
        "router_bias": nrm(ks[18], (N_EXPERTS,), 0.01),
        "moe_w_gate": nrm(ks[19], (DEPTH, N_EXPERTS, D_MODEL, EXPERT_FF), D_MODEL ** -0.5),
        "moe_w_up": nrm(ks[20], (DEPTH, N_EXPERTS, D_MODEL, EXPERT_FF), D_MODEL ** -0.5),
        "moe_w_down": nrm(ks[21], (DEPTH, N_EXPERTS, EXPERT_FF, D_MODEL), EXPERT_FF ** -0.5),
        "final_g": 1.0 + nrm(ks[22], (D_MODEL,), 0.02),
    }


def reference(x_prompt, x_sample, c, cache_k, cache_v, c_ctx, ada_w, ada_b, norm1_g, w_in, rpb,
              pool_w, pool_scale, w_branch_a, w_branch_p, w_out, norm2_g, router_w, router_bias,
              moe_w_gate, moe_w_up, moe_w_down, final_g):
    x = x_prompt
    B, L = x.shape[0], x.shape[1]
    new_ks = []
    new_vs = []
    for l in range(DEPTH):
        sh1, sc1, g1, sh2, sc2, g2 = ada_params(c_ctx, ada_w[l], ada_b[l])
        h = modulate(rmsnorm(x, norm1_g[l]), sh1, sc1)
        q, k, v, u, ga, gp = split_in(h @ w_in[l])
        q = q.reshape(B, L, NA_HEADS, NA_HEAD_DIM)
        k = k.reshape(B, L, NA_HEADS, NA_HEAD_DIM)
        v = v.reshape(B, L, NA_HEADS, NA_HEAD_DIM)
        new_ks.append(k)
        new_vs.append(v)
        o_a = context_attention(q, k, v)
        o_p = pool_mixer(u, pool_w[l], pool_scale[l])
        x = x + g1 * merge_branches(o_a, o_p, ga, gp, w_branch_a[l], w_branch_p[l], w_out[l])
        h2 = modulate(rmsnorm(x, norm2_g[l]), sh2, sc2)
        x = x + g2 * moe(h2, router_w, router_bias, moe_w_gate[l], moe_w_up[l], moe_w_down[l])
    y_prompt = rmsnorm(x, final_g)
    new_k = jnp.stack(new_ks, axis=1)
    new_v = jnp.stack(new_vs, axis=1)

    z = x_sample
    Bd, N = z.shape[0], z.shape[1]
    rows = N // GRID_W
    for l in range(DEPTH):
        sh1, sc1, g1, sh2, sc2, g2 = [m[:, None, :] for m in ada_params(c, ada_w[l], ada_b[l])]
        h = modulate(rmsnorm(z, norm1_g[l]), sh1, sc1)
        q, k, v, u, ga, gp = split_in(h @ w_in[l])
        q = q.reshape(Bd, N, NA_HEADS, NA_HEAD_DIM)
        k = k.reshape(Bd, N, NA_HEADS, NA_HEAD_DIM)
        v = v.reshape(Bd, N, NA_HEADS, NA_HEAD_DIM)
        o_a = neighbourhood_attention(q, k, v, cache_k[:, l], cache_v[:, l], rpb[l])
        o_p = pool_mixer(u.reshape(Bd, rows, GRID_W, POOL_WIDTH), pool_w[l], pool_scale[l])
        o_p = o_p.reshape(Bd, N, POOL_WIDTH)
        z = z + g1 * merge_branches(o_a, o_p, ga, gp, w_branch_a[l], w_branch_p[l], w_out[l])
        h2 = modulate(rmsnorm(z, norm2_g[l]), sh2, sc2)
        z = z + g2 * moe(h2, router_w, router_bias, moe_w_gate[l], moe_w_up[l], moe_w_down[l])
    y_sample = rmsnorm(z, final_g)

    return (y_prompt, y_sample, new_k, new_v)
```

```python
import functools

import numpy as np
import jax
import jax.numpy as jnp
from jax import lax
from jax.experimental import pallas as pl
from jax.experimental.pallas import tpu as pltpu

D_MODEL = 2048
BATCH = 32
SEQ = 256
DEPTH = 2
DEC_BATCH = 2
DEC_SEQ = 2048
PAST_LEN = 256
GRID_W = 64
GRID_ROWS = DEC_SEQ // GRID_W
NA_HEADS = 16
NA_HEAD_DIM = 64
NA_WIDTH = NA_HEADS * NA_HEAD_DIM
WIN_H = 8
WIN_W = 16
POOL_WIDTH = D_MODEL // 2
POOL_WINDOWS = (2, 4, 8, 16)
POOL_GROUPS = len(POOL_WINDOWS)
POOL_GC = POOL_WIDTH // POOL_GROUPS
IN_WIDTH = 3 * NA_WIDTH + POOL_WIDTH + 2 * D_MODEL
N_EXPERTS = 16
N_GROUPS = 4
EXPERTS_PER_GROUP = N_EXPERTS // N_GROUPS
EXPERT_FF = D_MODEL // 4
N_MOD = 6
EPS = 1e-6

T_CTX = BATCH * SEQ
T_SMP = DEC_BATCH * DEC_SEQ
T_ALL = T_CTX + T_SMP
N_COND = 8
NA_KEYS = WIN_H * GRID_W
NA_ROW_CLASSES = 8
MASKED = -1e30

F32 = jnp.float32
BF16 = jnp.bfloat16
VMEM_LIMIT = 56 * 1024 * 1024

SH1, SC1, G1, SH2, SC2, G2 = range(6)
ROW_CHUNK = 256


def _params(sem, vmem=VMEM_LIMIT):
    return pltpu.CompilerParams(dimension_semantics=sem, vmem_limit_bytes=vmem)


def _cond_of_tile(i, tm):
    t0 = i * tm
    return jnp.where(t0 < T_CTX, 0, 1 + (t0 - T_CTX) // DEC_SEQ)


def _rms(x):
    return x * lax.rsqrt(jnp.mean(x * x, axis=-1, keepdims=True) + EPS)


def _dot(a, b):
    return jnp.dot(a, b, preferred_element_type=F32)


def _dot_nt(a, b):
    return lax.dot_general(a, b, (((1,), (1,)), ((), ())), preferred_element_type=F32)


def _split3(x):
    x1 = x.astype(BF16)
    r1 = x - x1.astype(F32)
    x2 = r1.astype(BF16)
    x3 = (r1 - x2.astype(F32)).astype(BF16)
    return x1, x2, x3


def _row_to_col(v):
    n = v.shape[1]
    r = lax.broadcasted_iota(jnp.int32, (n, n), 0)
    c = lax.broadcasted_iota(jnp.int32, (n, n), 1)
    return jnp.sum(jnp.where(r == c, v, 0.0), axis=1, keepdims=True)


def _ada_kernel(c_ref, w_ref, b_ref, o_ref):
    s = jax.nn.silu(c_ref[...]).astype(BF16)
    o_ref[0] = _dot(s, w_ref[0].astype(BF16)) + b_ref[0]


def _ada(cond, ada_w, ada_b):
    tn = 1024
    nj = N_MOD * D_MODEL // tn
    return pl.pallas_call(
        _ada_kernel,
        grid=(DEPTH, nj),
        in_specs=[
            pl.BlockSpec((N_COND, D_MODEL), lambda l, j: (0, 0)),
            pl.BlockSpec((1, D_MODEL, tn), lambda l, j: (l, 0, j)),
            pl.BlockSpec((1, 1, tn), lambda l, j: (l, 0, j)),
        ],
        out_specs=pl.BlockSpec((1, N_COND, tn), lambda l, j: (l, 0, j)),
        out_shape=jax.ShapeDtypeStruct((DEPTH, N_COND, N_MOD * D_MODEL), F32),
        compiler_params=_params(("arbitrary", "arbitrary")),
        name="ada",
    )(cond, ada_w, ada_b.reshape(DEPTH, 1, N_MOD * D_MODEL))


def _inproj_kernel(x_ref, g_ref, mod_ref, w_ref, o_ref, h_ref):
    @pl.when(pl.program_id(1) == 0)
    def _():
        sh = mod_ref[0, 0, SH1:SH1 + 1, :]
        sc = mod_ref[0, 0, SC1:SC1 + 1, :]

        def chunk(ci, carry):
            rows = pl.ds(pl.multiple_of(ci * ROW_CHUNK, ROW_CHUNK), ROW_CHUNK)
            h_ref[rows, :] = (_rms(x_ref[rows, :]) * g_ref[0] * (1.0 + sc) + sh).astype(BF16)
            return carry

        lax.fori_loop(0, x_ref.shape[0] // ROW_CHUNK, chunk, 0)

    o_ref[...] = _dot(h_ref[...], w_ref[0].astype(BF16))


def _inproj(x, norm_g, mods, w_in, layer):
    tm, tn = 1024, 512
    return pl.pallas_call(
        _inproj_kernel,
        grid=(T_ALL // tm, IN_WIDTH // tn),
        in_specs=[
            pl.BlockSpec((tm, D_MODEL), lambda i, j: (i, 0)),
            pl.BlockSpec((1, 1, D_MODEL), lambda i, j: (layer, 0, 0)),
            pl.BlockSpec((1, 1, N_MOD, D_MODEL), lambda i, j: (layer, _cond_of_tile(i, tm), 0, 0)),
            pl.BlockSpec((1, D_MODEL, tn), lambda i, j: (layer, 0, j)),
        ],
        out_specs=pl.BlockSpec((tm, tn), lambda i, j: (i, j)),
        out_shape=jax.ShapeDtypeStruct((T_ALL, IN_WIDTH), F32),
        scratch_shapes=[pltpu.VMEM((tm, D_MODEL), BF16)],
        compiler_params=_params(("arbitrary", "arbitrary")),
        name=f"inproj{layer}",
    )(x, norm_g.reshape(DEPTH, 1, D_MODEL), mods, w_in)


def _pool_consts(n, seg):
    t = np.arange(n)
    pos = t % seg
    base = t - pos
    mats = np.zeros((POOL_GROUPS, n, n), np.float32)
    cnts = np.zeros((POOL_GROUPS, n, 1), np.float32)
    for gi, w in enumerate(POOL_WINDOWS):
        lo = np.clip(pos - w // 2, 0, seg)
        hi = np.clip(pos + w - w // 2, 0, seg)
        s = t[None, :]
        mats[gi] = ((s >= (base + lo)[:, None]) & (s < (base + hi)[:, None])).astype(np.float32)
        cnts[gi, :, 0] = hi - lo
    return jnp.asarray(mats, BF16), jnp.asarray(cnts, F32)


def _pool(u_ref, band_ref, cnt_ref, pw_ref, ps_ref, op_ref):
    for gi in range(POOL_GROUPS):
        cols = slice(gi * POOL_GC, (gi + 1) * POOL_GC)
        ug = u_ref[:, cols]
        u1 = ug.astype(BF16)
        u2 = (ug - u1.astype(F32)).astype(BF16)
        u3 = (ug - u1.astype(F32) - u2.astype(F32)).astype(BF16)
        band = band_ref[gi]
        wsum = _dot(band, u1) + _dot(band, u2) + _dot(band, u3)
        d = (wsum / cnt_ref[gi] - ug).astype(BF16)
        y = _dot(d, pw_ref[0, gi].astype(BF16))
        op_ref[:, cols] = (y * ps_ref[0, :, cols]).astype(BF16)


def _softmax_av(s_parts, v_parts):
    m = s_parts[0].max(axis=-1, keepdims=True)
    for s in s_parts[1:]:
        m = jnp.maximum(m, s.max(axis=-1, keepdims=True))
    den = None
    acc = None
    for s, v in zip(s_parts, v_parts):
        p = jnp.exp(s - m)
        ps = p.sum(axis=-1, keepdims=True)
        den = ps if den is None else den + ps
        pv = _dot(p.astype(BF16), v)
        acc = pv if acc is None else acc + pv
    return acc / den


def _ctx_mixer_kernel(q_ref, k_ref, v_ref, u_ref, band_ref, cnt_ref, pw_ref, ps_ref, oa_ref, op_ref):
    scale = NA_HEAD_DIM ** -0.5
    for hp in range(NA_HEADS // 2):
        outs = []
        for h in (2 * hp, 2 * hp + 1):
            cols = slice(h * NA_HEAD_DIM, (h + 1) * NA_HEAD_DIM)
            q = (q_ref[:, cols] * scale).astype(BF16)
            k = k_ref[:, cols].astype(BF16)
            v = v_ref[:, cols].astype(BF16)
            outs.append(_softmax_av([_dot_nt(q, k)], [v]))
        oa_ref[:, 2 * hp * NA_HEAD_DIM:(2 * hp + 2) * NA_HEAD_DIM] = jnp.concatenate(outs, axis=-1).astype(BF16)
    _pool(u_ref, band_ref, cnt_ref, pw_ref, ps_ref, op_ref)


def _ctx_mixer(proj, pool_w, pool_scale, layer):
    band, cnt = _pool_consts(SEQ, SEQ)
    blk = lambda c: pl.BlockSpec((SEQ, NA_WIDTH), lambda s, c=c: (s, c))
    return pl.pallas_call(
        _ctx_mixer_kernel,
        grid=(BATCH,),
        in_specs=[
            blk(0), blk(1), blk(2), blk(3),
            pl.BlockSpec((POOL_GROUPS, SEQ, SEQ), lambda s: (0, 0, 0)),
            pl.BlockSpec((POOL_GROUPS, SEQ, 1), lambda s: (0, 0, 0)),
            pl.BlockSpec((1, POOL_GROUPS, POOL_GC, POOL_GC), lambda s: (layer, 0, 0, 0)),
            pl.BlockSpec((1, 1, POOL_WIDTH), lambda s: (layer, 0, 0)),
        ],
        out_specs=[pl.BlockSpec((SEQ, NA_WIDTH), lambda s: (s, 0)),
                   pl.BlockSpec((SEQ, POOL_WIDTH), lambda s: (s, 0))],
        out_shape=[jax.ShapeDtypeStruct((T_ALL, NA_WIDTH), BF16),
                   jax.ShapeDtypeStruct((T_ALL, POOL_WIDTH), BF16)],
        compiler_params=_params(("arbitrary",)),
        name=f"ctx_mixer{layer}",
    )(proj, proj, proj, proj, band, cnt, pool_w, pool_scale.reshape(DEPTH, 1, POOL_WIDTH))


def _na_bias_table(rpb):
    rows = np.array([0, 1, 2, 3, 4, 29, 30, 31])
    rs = np.clip(rows - WIN_H // 2, 0, GRID_ROWS - WIN_H)
    dr = rs[:, None] + np.arange(WIN_H)[None, :] - rows[:, None] + (WIN_H - 1)
    qc = np.arange(GRID_W)
    kc = np.arange(GRID_W)
    cstart = np.clip(qc - WIN_W // 2, 0, GRID_W - WIN_W)
    valid = (kc[None, :] >= cstart[:, None]) & (kc[None, :] < cstart[:, None] + WIN_W)
    dc = np.clip(kc[None, :] - qc[:, None] + (WIN_W - 1), 0, 2 * WIN_W - 2)
    b = rpb[:, :, dr[:, None, :, None], dc[None, :, None, :]]
    b = jnp.where(valid[None, None, None, :, None, :], b.astype(F32), MASKED)
    return b.transpose(0, 2, 1, 3, 4, 5).reshape(DEPTH, NA_ROW_CLASSES, NA_HEADS, GRID_W, NA_KEYS)


def _row_class(r):
    return jnp.where(r < 4, r, jnp.where(r <= GRID_ROWS - 4, 4, r - (GRID_ROWS - NA_ROW_CLASSES)))


def _na_mixer_kernel(q_ref, k_ref, v_ref, ck_ref, cv_ref, bias_ref, u_ref, band_ref, cnt_ref, pw_ref, ps_ref,
                     oa_in, op_in, oa_ref, op_ref):
    del oa_in, op_in
    r = pl.program_id(1)
    rs = jnp.clip(r - WIN_H // 2, 0, GRID_ROWS - WIN_H)
    keys = pl.ds(pl.multiple_of(rs * GRID_W, GRID_W), NA_KEYS)
    scale = NA_HEAD_DIM ** -0.5
    for hp in range(NA_HEADS // 2):
        outs = []
        for h in (2 * hp, 2 * hp + 1):
            cols = slice(h * NA_HEAD_DIM, (h + 1) * NA_HEAD_DIM)
            q = (q_ref[:, cols] * scale).astype(BF16)
            k = k_ref[keys, cols].astype(BF16)
            v = v_ref[keys, cols].astype(BF16)
            ck = ck_ref[0, 0, :, cols].astype(BF16)
            cv = cv_ref[0, 0, :, cols].astype(BF16)
            s_loc = _dot_nt(q, k) + bias_ref[0, 0, h]
            s_ctx = _dot_nt(q, ck)
            outs.append(_softmax_av([s_loc, s_ctx], [v, cv]))
        oa_ref[:, 2 * hp * NA_HEAD_DIM:(2 * hp + 2) * NA_HEAD_DIM] = jnp.concatenate(outs, axis=-1).astype(BF16)
    _pool(u_ref, band_ref, cnt_ref, pw_ref, ps_ref, op_ref)


def _na_mixer(proj, cache_k, cache_v, bias_tbl, pool_w, pool_scale, oa, op, layer):
    band, cnt = _pool_consts(GRID_W, GRID_W)
    row0 = T_CTX // GRID_W
    bat0 = T_CTX // DEC_SEQ
    qblk = lambda c: pl.BlockSpec((GRID_W, NA_WIDTH), lambda b, r, c=c: (row0 + b * GRID_ROWS + r, c))
    kvblk = lambda c: pl.BlockSpec((DEC_SEQ, NA_WIDTH), lambda b, r, c=c: (bat0 + b, c))
    cblk = pl.BlockSpec((1, 1, PAST_LEN, NA_WIDTH), lambda b, r: (b, layer, 0, 0))
    oblk = pl.BlockSpec((GRID_W, NA_WIDTH), lambda b, r: (row0 + b * GRID_ROWS + r, 0))
    return pl.pallas_call(
        _na_mixer_kernel,
        grid=(DEC_BATCH, GRID_ROWS),
        in_specs=[
            qblk(0), kvblk(1), kvblk(2), cblk, cblk,
            pl.BlockSpec((1, 1, NA_HEADS, GRID_W, NA_KEYS), lambda b, r: (layer, _row_class(r), 0, 0, 0)),
            qblk(3),
            pl.BlockSpec((POOL_GROUPS, GRID_W, GRID_W), lambda b, r: (0, 0, 0)),
            pl.BlockSpec((POOL_GROUPS, GRID_W, 1), lambda b, r: (0, 0, 0)),
            pl.BlockSpec((1, POOL_GROUPS, POOL_GC, POOL_GC), lambda b, r: (layer, 0, 0, 0)),
            pl.BlockSpec((1, 1, POOL_WIDTH), lambda b, r: (layer, 0, 0)),
            pl.BlockSpec(memory_space=pl.ANY),
            pl.BlockSpec(memory_space=pl.ANY),
        ],
        out_specs=[oblk, oblk],
        out_shape=[jax.ShapeDtypeStruct((T_ALL, NA_WIDTH), BF16),
                   jax.ShapeDtypeStruct((T_ALL, POOL_WIDTH), BF16)],
        input_output_aliases={11: 0, 12: 1},
        compiler_params=_params(("arbitrary", "arbitrary")),
        name=f"na_mixer{layer}",
    )(proj, proj, proj,
      cache_k.reshape(DEC_BATCH, DEPTH, PAST_LEN, NA_WIDTH), cache_v.reshape(DEC_BATCH, DEPTH, PAST_LEN, NA_WIDTH),
      bias_tbl, proj, band, cnt, pool_w, pool_scale.reshape(DEPTH, 1, POOL_WIDTH), oa, op)


def _merge_kernel(oa_ref, op_ref, ga_ref, gp_ref, x_ref, mod_ref, wba_ref, wbp_ref, wout_ref, n2g_ref, rwt_ref,
                  xo_ref, h2_ref, lg_ref):
    a = _dot(oa_ref[...], wba_ref[0])
    p = _dot(op_ref[...], wbp_ref[0])
    mix = jax.nn.sigmoid(ga_ref[...]) * a + jax.nn.sigmoid(gp_ref[...]) * p
    m = _dot(mix.astype(BF16), wout_ref[0])
    x = x_ref[...] + mod_ref[0, 0, G1:G1 + 1, :] * m
    xo_ref[...] = x
    h2 = _rms(x) * n2g_ref[0] * (1.0 + mod_ref[0, 0, SC2:SC2 + 1, :]) + mod_ref[0, 0, SH2:SH2 + 1, :]
    h2_ref[...] = h2.astype(BF16)
    w1, w2, w3 = _split3(rwt_ref[...])
    t1, t2, t3 = _split3(h2)
    lg_ref[...] = (_dot_nt(w1, t1) + _dot_nt(w1, t2) + _dot_nt(w2, t1)
                   + _dot_nt(w2, t2) + _dot_nt(w1, t3) + _dot_nt(w3, t1))


def _merge(oa, op, proj, x, mods, w_ba, w_bp, w_out, norm2_g, router_wt, layer):
    tm = 256
    full = lambda shape: pl.BlockSpec(shape, lambda i: (layer,) + (0,) * (len(shape) - 1))
    return pl.pallas_call(
        _merge_kernel,
        grid=(T_ALL // tm,),
        in_specs=[
            pl.BlockSpec((tm, NA_WIDTH), lambda i: (i, 0)),
            pl.BlockSpec((tm, POOL_WIDTH), lambda i: (i, 0)),
            pl.BlockSpec((tm, D_MODEL), lambda i: (i, 2)),
            pl.BlockSpec((tm, D_MODEL), lambda i: (i, 3)),
            pl.BlockSpec((tm, D_MODEL), lambda i: (i, 0)),
            pl.BlockSpec((1, 1, N_MOD, D_MODEL), lambda i: (layer, _cond_of_tile(i, tm), 0, 0)),
            full((1, NA_WIDTH, D_MODEL)),
            full((1, POOL_WIDTH, D_MODEL)),
            full((1, D_MODEL, D_MODEL)),
            full((1, 1, D_MODEL)),
            pl.BlockSpec((N_EXPERTS, D_MODEL), lambda i: (0, 0)),
        ],
        out_specs=[pl.BlockSpec((tm, D_MODEL), lambda i: (i, 0)),
                   pl.BlockSpec((tm, D_MODEL), lambda i: (i, 0)),
                   pl.BlockSpec((N_EXPERTS, tm), lambda i: (0, i))],
        out_shape=[jax.ShapeDtypeStruct((T_ALL, D_MODEL), F32),
                   jax.ShapeDtypeStruct((T_ALL, D_MODEL), BF16),
                   jax.ShapeDtypeStruct((N_EXPERTS, T_ALL), F32)],
        compiler_params=_params(("arbitrary",)),
        name=f"merge{layer}",
    )(oa, op, proj, proj, x, mods, w_ba, w_bp, w_out, norm2_g.reshape(DEPTH, 1, D_MODEL), router_wt)


def _route_kernel(lg_ref, rb_ref, gt_ref):
    scores = jax.nn.sigmoid(lg_ref[...])
    sel = scores + rb_ref[...]
    gscore = []
    for g in range(N_GROUPS):
        a, b, c, d = [sel[EXPERTS_PER_GROUP * g + i:EXPERTS_PER_GROUP * g + i + 1, :] for i in range(4)]
        hi1, lo1, hi2, lo2 = jnp.maximum(a, b), jnp.minimum(a, b), jnp.maximum(c, d), jnp.minimum(c, d)
        gscore.append(jnp.maximum(hi1, hi2) + jnp.maximum(jnp.minimum(hi1, hi2), jnp.maximum(lo1, lo2)))
    best = jnp.zeros_like(gscore[0], dtype=jnp.int32)
    bestv = gscore[0]
    for g in range(1, N_GROUPS):
        upd = gscore[g] > bestv
        best = jnp.where(upd, g, best)
        bestv = jnp.where(upd, gscore[g], bestv)
    row = lax.broadcasted_iota(jnp.int32, sel.shape, 0)
    masked = jnp.where(row // EXPERTS_PER_GROUP == best, sel, -jnp.inf)
    i0 = jnp.min(jnp.where(masked == masked.max(axis=0, keepdims=True), row, N_EXPERTS), axis=0, keepdims=True)
    masked = jnp.where(row == i0, -jnp.inf, masked)
    i1 = jnp.min(jnp.where(masked == masked.max(axis=0, keepdims=True), row, N_EXPERTS), axis=0, keepdims=True)
    s0 = jnp.sum(jnp.where(row == i0, scores, 0.0), axis=0, keepdims=True)
    s1 = jnp.sum(jnp.where(row == i1, scores, 0.0), axis=0, keepdims=True)
    den = s0 + s1
    gt_ref[...] = jnp.where(row == i0, s0 / den, 0.0) + jnp.where(row == i1, s1 / den, 0.0)


def _route(logits_t, router_bias):
    tn = 1024
    return pl.pallas_call(
        _route_kernel,
        grid=(T_ALL // tn,),
        in_specs=[pl.BlockSpec((N_EXPERTS, tn), lambda i: (0, i)),
                  pl.BlockSpec((N_EXPERTS, 1), lambda i: (0, 0))],
        out_specs=pl.BlockSpec((N_EXPERTS, tn), lambda i: (0, i)),
        out_shape=jax.ShapeDtypeStruct((N_EXPERTS, T_ALL), F32),
        compiler_params=_params(("arbitrary",)),
        name="route",
    )(logits_t, router_bias.reshape(N_EXPERTS, 1).astype(F32))


def _moe_kernel(h_ref, gt_ref, wg_ref, wu_ref, wd_ref, x_ref, mod_ref, o_ref, acc_ref):
    e = pl.program_id(1)

    @pl.when(e == 0)
    def _():
        acc_ref[...] = jnp.zeros_like(acc_ref)

    h = h_ref[...]
    g = _dot(h, wg_ref[0, 0].astype(BF16))
    u = _dot(h, wu_ref[0, 0].astype(BF16))
    gate = _row_to_col(gt_ref[pl.ds(e, 1), :])
    act = jax.nn.silu(g) * u * gate
    acc_ref[...] += _dot(act.astype(BF16), wd_ref[0, 0].astype(BF16))

    @pl.when(e == N_EXPERTS - 1)
    def _():
        o_ref[...] = x_ref[...] + mod_ref[0, 0, G2:G2 + 1, :] * acc_ref[...]


def _moe(h2, gates_t, w_gate, w_up, w_down, x, mods, layer):
    tm = 256
    return pl.pallas_call(
        _moe_kernel,
        grid=(T_ALL // tm, N_EXPERTS),
        in_specs=[
            pl.BlockSpec((tm, D_MODEL), lambda i, e: (i, 0)),
            pl.BlockSpec((N_EXPERTS, tm), lambda i, e: (0, i)),
            pl.BlockSpec((1, 1, D_MODEL, EXPERT_FF), lambda i, e: (layer, e, 0, 0)),
            pl.BlockSpec((1, 1, D_MODEL, EXPERT_FF), lambda i, e: (layer, e, 0, 0)),
            pl.BlockSpec((1, 1, EXPERT_FF, D_MODEL), lambda i, e: (layer, e, 0, 0)),
            pl.BlockSpec((tm, D_MODEL), lambda i, e: (i, 0)),
            pl.BlockSpec((1, 1, N_MOD, D_MODEL), lambda i, e: (layer, _cond_of_tile(i, tm), 0, 0)),
        ],
        out_specs=pl.BlockSpec((tm, D_MODEL), lambda i, e: (i, 0)),
        out_shape=jax.ShapeDtypeStruct((T_ALL, D_MODEL), F32),
        scratch_shapes=[pltpu.VMEM((tm, D_MODEL), F32)],
        compiler_params=_params(("arbitrary", "arbitrary")),
        name=f"moe{layer}",
    )(h2, gates_t, w_gate, w_up, w_down, x, mods)


def _final_kernel(x_ref, g_ref, o_ref):
    o_ref[...] = _rms(x_ref[...]) * g_ref[...]


def _final(x, final_g):
    tm = 512
    return pl.pallas_call(
        _final_kernel,
        grid=(T_ALL // tm,),
        in_specs=[pl.BlockSpec((tm, D_MODEL), lambda i: (i, 0)),
                  pl.BlockSpec((1, D_MODEL), lambda i: (0, 0))],
        out_specs=pl.BlockSpec((tm, D_MODEL), lambda i: (i, 0)),
        out_shape=jax.ShapeDtypeStruct((T_ALL, D_MODEL), F32),
        compiler_params=_params(("arbitrary",)),
        name="final_norm",
    )(x, final_g.reshape(1, D_MODEL))


def kernel(x_prompt, x_sample, c, cache_k, cache_v, c_ctx, ada_w, ada_b, norm1_g, w_in, rpb, pool_w, pool_scale,
           w_branch_a, w_branch_p, w_out, norm2_g, router_w, router_bias, moe_w_gate, moe_w_up, moe_w_down, final_g):
    x = jnp.concatenate([x_prompt.reshape(T_CTX, D_MODEL), x_sample.reshape(T_SMP, D_MODEL)], axis=0)
    cond = jnp.concatenate([c_ctx[None, :], c, jnp.zeros((N_COND - 1 - DEC_BATCH, D_MODEL), F32)], axis=0)
    mods = _ada(cond, ada_w, ada_b).reshape(DEPTH, N_COND, N_MOD, D_MODEL)
    bias_tbl = _na_bias_table(rpb)
    w_ba, w_bp, w_o = w_branch_a.astype(BF16), w_branch_p.astype(BF16), w_out.astype(BF16)
    router_wt = router_w.T

    new_ks, new_vs = [], []
    for l in range(DEPTH):
        proj = _inproj(x, norm1_g, mods, w_in, l)
        new_ks.append(proj[:T_CTX, NA_WIDTH:2 * NA_WIDTH].reshape(BATCH, SEQ, NA_HEADS, NA_HEAD_DIM))
        new_vs.append(proj[:T_CTX, 2 * NA_WIDTH:3 * NA_WIDTH].reshape(BATCH, SEQ, NA_HEADS, NA_HEAD_DIM))
        oa, op = _ctx_mixer(proj, pool_w, pool_scale, l)
        oa, op = _na_mixer(proj, cache_k, cache_v, bias_tbl, pool_w, pool_scale, oa, op, l)
        x, h2, logits_t = _merge(oa, op, proj, x, mods, w_ba, w_bp, w_o, norm2_g, router_wt, l)
        gates_t = _route(logits_t, router_bias)
        x = _moe(h2, gates_t, moe_w_gate, moe_w_up, moe_w_down, x, mods, l)
    y = _final(x, final_g)
    y_prompt = y[:T_CTX].reshape(BATCH, SEQ, D_MODEL)
    y_sample = y[T_CTX:].reshape(DEC_BATCH, DEC_SEQ, D_MODEL)
    return (y_prompt, y_sample, jnp.stack(new_ks, axis=1), jnp.stack(new_vs, axis=1))
```

```python
import functools

import numpy as np
import jax
import jax.numpy as jnp
from jax import lax
from jax.experimental import pallas as pl
from jax.experimental.pallas import tpu as pltpu

D_MODEL = 2048
BATCH = 32
SEQ = 256
DEPTH = 2
DEC_BATCH = 2
DEC_SEQ = 2048
PAST_LEN = 256
GRID_W = 64
GRID_ROWS = DEC_SEQ // GRID_W
NA_HEADS = 16
NA_HEAD_DIM = 64
NA_WIDTH = NA_HEADS * NA_HEAD_DIM
WIN_H = 8
WIN_W = 16
POOL_WIDTH = D_MODEL // 2
POOL_WINDOWS = (2, 4, 8, 16)
POOL_GROUPS = len(POOL_WINDOWS)
POOL_GC = POOL_WIDTH // POOL_GROUPS
IN_WIDTH = 3 * NA_WIDTH + POOL_WIDTH + 2 * D_MODEL
N_EXPERTS = 16
N_GROUPS = 4
EXPERTS_PER_GROUP = N_EXPERTS // N_GROUPS
TOP_K = 2
EXPERT_FF = D_MODEL // 4
N_MOD = 6
EPS = 1e-6

T_CTX = BATCH * SEQ
T_SMP = DEC_BATCH * DEC_SEQ
T_ALL = T_CTX + T_SMP
N_COND = 8
NA_KEYS = WIN_H * GRID_W
MASKED = -1e30

F32 = jnp.float32
BF16 = jnp.bfloat16
VMEM_LIMIT = 56 * 1024 * 1024

SH1, SC1, G1, SH2, SC2, G2 = range(6)
ROW_CHUNK = 256
LANES = 128
ROUTE_BLOCK = 512
MOE_TM = 256
MOE_TILES = T_ALL * TOP_K // MOE_TM + N_EXPERTS
COMBINE_TM = 256


def _params(sem, vmem=VMEM_LIMIT):
    return pltpu.CompilerParams(dimension_semantics=sem, vmem_limit_bytes=vmem)


def _cond_of_tile(i, tm):
    t0 = i * tm
    return jnp.where(t0 < T_CTX, 0, 1 + (t0 - T_CTX) // DEC_SEQ)


def _rms(x):
    return x * lax.rsqrt(jnp.mean(x * x, axis=-1, keepdims=True) + EPS)


def _dot(a, b):
    return jnp.dot(a, b, preferred_element_type=F32)


def _dot_nt(a, b):
    return lax.dot_general(a, b, (((1,), (1,)), ((), ())), preferred_element_type=F32)


def _split3(x):
    x1 = x.astype(BF16)
    r1 = x - x1.astype(F32)
    x2 = r1.astype(BF16)
    x3 = (r1 - x2.astype(F32)).astype(BF16)
    return x1, x2, x3


def _row_to_col(v):
    n = v.shape[1]
    r = lax.broadcasted_iota(jnp.int32, (n, n), 0)
    c = lax.broadcasted_iota(jnp.int32, (n, n), 1)
    return jnp.sum(jnp.where(r == c, v, 0.0), axis=1, keepdims=True)


def _ada_kernel(c_ref, w_ref, b_ref, o_ref):
    s = jax.nn.silu(c_ref[...]).astype(BF16)
    o_ref[0] = _dot(s, w_ref[0].astype(BF16)) + b_ref[0]


def _ada(cond, ada_w, ada_b):
    tn = 1024
    nj = N_MOD * D_MODEL // tn
    return pl.pallas_call(
        _ada_kernel,
        grid=(DEPTH, nj),
        in_specs=[
            pl.BlockSpec((N_COND, D_MODEL), lambda l, j: (0, 0)),
            pl.BlockSpec((1, D_MODEL, tn), lambda l, j: (l, 0, j)),
            pl.BlockSpec((1, 1, tn), lambda l, j: (l, 0, j)),
        ],
        out_specs=pl.BlockSpec((1, N_COND, tn), lambda l, j: (l, 0, j)),
        out_shape=jax.ShapeDtypeStruct((DEPTH, N_COND, N_MOD * D_MODEL), F32),
        compiler_params=_params(("arbitrary", "arbitrary")),
        name="ada",
    )(cond, ada_w, ada_b.reshape(DEPTH, 1, N_MOD * D_MODEL))


def _inproj_kernel(x_ref, g_ref, mod_ref, w_ref, o_ref, h_ref):
    @pl.when(pl.program_id(1) == 0)
    def _():
        sh = mod_ref[0, 0, SH1:SH1 + 1, :]
        sc = mod_ref[0, 0, SC1:SC1 + 1, :]

        def chunk(ci, carry):
            rows = pl.ds(pl.multiple_of(ci * ROW_CHUNK, ROW_CHUNK), ROW_CHUNK)
            h_ref[rows, :] = (_rms(x_ref[rows, :]) * g_ref[0] * (1.0 + sc) + sh).astype(BF16)
            return carry

        lax.fori_loop(0, x_ref.shape[0] // ROW_CHUNK, chunk, 0)

    o_ref[...] = _dot(h_ref[...], w_ref[0].astype(BF16))


def _inproj(x, norm_g, mods, w_in, layer):
    tm, tn = 1024, 512
    return pl.pallas_call(
        _inproj_kernel,
        grid=(T_ALL // tm, IN_WIDTH // tn),
        in_specs=[
            pl.BlockSpec((tm, D_MODEL), lambda i, j: (i, 0)),
            pl.BlockSpec((1, 1, D_MODEL), lambda i, j: (layer, 0, 0)),
            pl.BlockSpec((1, 1, N_MOD, D_MODEL), lambda i, j: (layer, _cond_of_tile(i, tm), 0, 0)),
            pl.BlockSpec((1, D_MODEL, tn), lambda i, j: (layer, 0, j)),
        ],
        out_specs=pl.BlockSpec((tm, tn), lambda i, j: (i, j)),
        out_shape=jax.ShapeDtypeStruct((T_ALL, IN_WIDTH), F32),
        scratch_shapes=[pltpu.VMEM((tm, D_MODEL), BF16)],
        compiler_params=_params(("arbitrary", "arbitrary")),
        name=f"inproj{layer}",
    )(x, norm_g.reshape(DEPTH, 1, D_MODEL), mods, w_in)


def _pool_consts(n, seg):
    t = np.arange(n)
    pos = t % seg
    base = t - pos
    mats = np.zeros((POOL_GROUPS, n, n), np.float32)
    cnts = np.zeros((POOL_GROUPS, n, 1), np.float32)
    for gi, w in enumerate(POOL_WINDOWS):
        lo = np.clip(pos - w // 2, 0, seg)
        hi = np.clip(pos + w - w // 2, 0, seg)
        s = t[None, :]
        mats[gi] = ((s >= (base + lo)[:, None]) & (s < (base + hi)[:, None])).astype(np.float32)
        cnts[gi, :, 0] = hi - lo
    return jnp.asarray(mats, BF16), jnp.asarray(cnts, F32)


def _pool(u_ref, band_ref, cnt_ref, pw_ref, ps_ref, op_ref):
    for gi in range(POOL_GROUPS):
        cols = slice(gi * POOL_GC, (gi + 1) * POOL_GC)
        ug = u_ref[:, cols]
        u1 = ug.astype(BF16)
        u2 = (ug - u1.astype(F32)).astype(BF16)
        u3 = (ug - u1.astype(F32) - u2.astype(F32)).astype(BF16)
        band = band_ref[gi]
        wsum = _dot(band, u1) + _dot(band, u2) + _dot(band, u3)
        d = (wsum / cnt_ref[gi] - ug).astype(BF16)
        y = _dot(d, pw_ref[0, gi].astype(BF16))
        op_ref[:, cols] = (y * ps_ref[0, :, cols]).astype(BF16)


def _softmax_av(s_parts, v_parts):
    m = s_parts[0].max(axis=-1, keepdims=True)
    for s in s_parts[1:]:
        m = jnp.maximum(m, s.max(axis=-1, keepdims=True))
    den = None
    acc = None
    for s, v in zip(s_parts, v_parts):
        p = jnp.exp(s - m)
        ps = p.sum(axis=-1, keepdims=True)
        den = ps if den is None else den + ps
        pv = _dot(p.astype(BF16), v)
        acc = pv if acc is None else acc + pv
    return acc / den


def _ctx_mixer_kernel(q_ref, k_ref, v_ref, u_ref, band_ref, cnt_ref, pw_ref, ps_ref, oa_ref, op_ref):
    scale = NA_HEAD_DIM ** -0.5
    for hp in range(NA_HEADS // 2):
        outs = []
        for h in (2 * hp, 2 * hp + 1):
            cols = slice(h * NA_HEAD_DIM, (h + 1) * NA_HEAD_DIM)
            q = (q_ref[:, cols] * scale).astype(BF16)
            k = k_ref[:, cols].astype(BF16)
            v = v_ref[:, cols].astype(BF16)
            outs.append(_softmax_av([_dot_nt(q, k)], [v]))
        oa_ref[:, 2 * hp * NA_HEAD_DIM:(2 * hp + 2) * NA_HEAD_DIM] = jnp.concatenate(outs, axis=-1).astype(BF16)
    _pool(u_ref, band_ref, cnt_ref, pw_ref, ps_ref, op_ref)


def _ctx_mixer(proj, pool_w, pool_scale, layer):
    band, cnt = _pool_consts(SEQ, SEQ)
    blk = lambda c: pl.BlockSpec((SEQ, NA_WIDTH), lambda s, c=c: (s, c))
    return pl.pallas_call(
        _ctx_mixer_kernel,
        grid=(BATCH,),
        in_specs=[
            blk(0), blk(1), blk(2), blk(3),
            pl.BlockSpec((POOL_GROUPS, SEQ, SEQ), lambda s: (0, 0, 0)),
            pl.BlockSpec((POOL_GROUPS, SEQ, 1), lambda s: (0, 0, 0)),
            pl.BlockSpec((1, POOL_GROUPS, POOL_GC, POOL_GC), lambda s: (layer, 0, 0, 0)),
            pl.BlockSpec((1, 1, POOL_WIDTH), lambda s: (layer, 0, 0)),
        ],
        out_specs=[pl.BlockSpec((SEQ, NA_WIDTH), lambda s: (s, 0)),
                   pl.BlockSpec((SEQ, POOL_WIDTH), lambda s: (s, 0))],
        out_shape=[jax.ShapeDtypeStruct((T_CTX, NA_WIDTH), BF16),
                   jax.ShapeDtypeStruct((T_CTX, POOL_WIDTH), BF16)],
        compiler_params=_params(("arbitrary",)),
        name=f"ctx_mixer{layer}",
    )(proj, proj, proj, proj, band, cnt, pool_w, pool_scale.reshape(DEPTH, 1, POOL_WIDTH))


def _na_build_bias(rpb_ref, bias_ref, r, rs):
    lanes = 2 * GRID_W
    qc = lax.broadcasted_iota(jnp.int32, (GRID_W, lanes), 0)
    lane = lax.broadcasted_iota(jnp.int32, (GRID_W, lanes), 1)
    kc = jnp.bitwise_and(lane, GRID_W - 1)
    cstart = jnp.clip(qc - WIN_W // 2, 0, GRID_W - WIN_W)
    valid = (kc >= cstart) & (kc < cstart + WIN_W)
    for h in range(NA_HEADS):
        for jp in range(WIN_H // 2):
            halves = []
            for j, shift in ((2 * jp, lanes - (WIN_W - 1)), (2 * jp + 1, GRID_W - (WIN_W - 1))):
                dr = rs + j - r + (WIN_H - 1)
                row = jnp.broadcast_to(rpb_ref[0, h, pl.ds(dr, 1), :], (GRID_W, lanes))
                halves.append(pltpu.roll(row, shift, 1, stride=1, stride_axis=0))
            tile = jnp.where(lane < GRID_W, halves[0], halves[1])
            bias_ref[h, :, jp * lanes:(jp + 1) * lanes] = jnp.where(valid, tile, MASKED)


def _na_mixer_kernel(q_ref, k_ref, v_ref, ck_ref, cv_ref, rpb_ref, u_ref, band_ref, cnt_ref, pw_ref, ps_ref,
                     oa_ref, op_ref, bias_ref):
    r = pl.program_id(1)
    rs = jnp.clip(r - WIN_H // 2, 0, GRID_ROWS - WIN_H)

    @pl.when((r <= WIN_H // 2) | (r > GRID_ROWS - WIN_H // 2))
    def _():
        _na_build_bias(rpb_ref, bias_ref, r, rs)

    keys = pl.ds(pl.multiple_of(rs * GRID_W, GRID_W), NA_KEYS)
    scale = NA_HEAD_DIM ** -0.5
    for hp in range(NA_HEADS // 2):
        outs = []
        for h in (2 * hp, 2 * hp + 1):
            cols = slice(h * NA_HEAD_DIM, (h + 1) * NA_HEAD_DIM)
            q = (q_ref[:, cols] * scale).astype(BF16)
            k = k_ref[keys, cols].astype(BF16)
            v = v_ref[keys, cols].astype(BF16)
            ck = ck_ref[0, 0, :, cols].astype(BF16)
            cv = cv_ref[0, 0, :, cols].astype(BF16)
            s_loc = _dot_nt(q, k) + bias_ref[h]
            s_ctx = _dot_nt(q, ck)
            outs.append(_softmax_av([s_loc, s_ctx], [v, cv]))
        oa_ref[:, 2 * hp * NA_HEAD_DIM:(2 * hp + 2) * NA_HEAD_DIM] = jnp.concatenate(outs, axis=-1).astype(BF16)
    _pool(u_ref, band_ref, cnt_ref, pw_ref, ps_ref, op_ref)


def _na_mixer(proj, cache_k, cache_v, rpb_pad, pool_w, pool_scale, layer):
    band, cnt = _pool_consts(GRID_W, GRID_W)
    row0 = T_CTX // GRID_W
    bat0 = T_CTX // DEC_SEQ
    qblk = lambda c: pl.BlockSpec((GRID_W, NA_WIDTH), lambda b, r, c=c: (row0 + b * GRID_ROWS + r, c))
    kvblk = lambda c: pl.BlockSpec((DEC_SEQ, NA_WIDTH), lambda b, r, c=c: (bat0 + b, c))
    cblk = pl.BlockSpec((1, 1, PAST_LEN, NA_WIDTH), lambda b, r: (b, layer, 0, 0))
    oblk = pl.BlockSpec((GRID_W, NA_WIDTH), lambda b, r: (b * GRID_ROWS + r, 0))
    return pl.pallas_call(
        _na_mixer_kernel,
        grid=(DEC_BATCH, GRID_ROWS),
        in_specs=[
            qblk(0), kvblk(1), kvblk(2), cblk, cblk,
            pl.BlockSpec((1,) + rpb_pad.shape[1:], lambda b, r: (layer, 0, 0, 0)),
            qblk(3),
            pl.BlockSpec((POOL_GROUPS, GRID_W, GRID_W), lambda b, r: (0, 0, 0)),
            pl.BlockSpec((POOL_GROUPS, GRID_W, 1), lambda b, r: (0, 0, 0)),
            pl.BlockSpec((1, POOL_GROUPS, POOL_GC, POOL_GC), lambda b, r: (layer, 0, 0, 0)),
            pl.BlockSpec((1, 1, POOL_WIDTH), lambda b, r: (layer, 0, 0)),
        ],
        out_specs=[oblk, oblk],
        out_shape=[jax.ShapeDtypeStruct((T_SMP, NA_WIDTH), BF16),
                   jax.ShapeDtypeStruct((T_SMP, POOL_WIDTH), BF16)],
        scratch_shapes=[pltpu.VMEM((NA_HEADS, GRID_W, NA_KEYS), F32)],
        compiler_params=_params(("arbitrary", "arbitrary")),
        name=f"na_mixer{layer}",
    )(proj, proj, proj,
      cache_k.reshape(DEC_BATCH, DEPTH, PAST_LEN, NA_WIDTH), cache_v.reshape(DEC_BATCH, DEPTH, PAST_LEN, NA_WIDTH),
      rpb_pad, proj, band, cnt, pool_w, pool_scale.reshape(DEPTH, 1, POOL_WIDTH))


def _merge_kernel(oac_ref, opc_ref, oas_ref, ops_ref, ga_ref, gp_ref, x_ref, mod_ref, wba_ref, wbp_ref, wout_ref,
                  n2g_ref, rwt_ref, xo_ref, h2_ref, lg_ref):
    is_ctx = pl.program_id(0) < T_CTX // x_ref.shape[0]
    a = _dot(jnp.where(is_ctx, oac_ref[...], oas_ref[...]), wba_ref[0])
    p = _dot(jnp.where(is_ctx, opc_ref[...], ops_ref[...]), wbp_ref[0])
    mix = jax.nn.sigmoid(ga_ref[...]) * a + jax.nn.sigmoid(gp_ref[...]) * p
    m = _dot(mix.astype(BF16), wout_ref[0])
    x = x_ref[...] + mod_ref[0, 0, G1:G1 + 1, :] * m
    xo_ref[...] = x
    h2 = _rms(x) * n2g_ref[0] * (1.0 + mod_ref[0, 0, SC2:SC2 + 1, :]) + mod_ref[0, 0, SH2:SH2 + 1, :]
    h2_ref[...] = h2
    w1, w2, w3 = _split3(rwt_ref[...])
    t1, t2, t3 = _split3(h2)
    lg_ref[...] = (_dot_nt(w1, t1) + _dot_nt(w1, t2) + _dot_nt(w2, t1)
                   + _dot_nt(w2, t2) + _dot_nt(w1, t3) + _dot_nt(w3, t1))


def _merge(oa_c, op_c, oa_s, op_s, proj, x, mods, w_ba, w_bp, w_out, norm2_g, router_wt, layer):
    tm = 256
    nc = T_CTX // tm
    full = lambda shape: pl.BlockSpec(shape, lambda i: (layer,) + (0,) * (len(shape) - 1))
    return pl.pallas_call(
        _merge_kernel,
        grid=(T_ALL // tm,),
        in_specs=[
            pl.BlockSpec((tm, NA_WIDTH), lambda i: (jnp.minimum(i, nc - 1), 0)),
            pl.BlockSpec((tm, POOL_WIDTH), lambda i: (jnp.minimum(i, nc - 1), 0)),
            pl.BlockSpec((tm, NA_WIDTH), lambda i: (jnp.maximum(i - nc, 0), 0)),
            pl.BlockSpec((tm, POOL_WIDTH), lambda i: (jnp.maximum(i - nc, 0), 0)),
            pl.BlockSpec((tm, D_MODEL), lambda i: (i, 2)),
            pl.BlockSpec((tm, D_MODEL), lambda i: (i, 3)),
            pl.BlockSpec((tm, D_MODEL), lambda i: (i, 0)),
            pl.BlockSpec((1, 1, N_MOD, D_MODEL), lambda i: (layer, _cond_of_tile(i, tm), 0, 0)),
            full((1, NA_WIDTH, D_MODEL)),
            full((1, POOL_WIDTH, D_MODEL)),
            full((1, D_MODEL, D_MODEL)),
            full((1, 1, D_MODEL)),
            pl.BlockSpec((N_EXPERTS, D_MODEL), lambda i: (0, 0)),
        ],
        out_specs=[pl.BlockSpec((tm, D_MODEL), lambda i: (i, 0)),
                   pl.BlockSpec((tm, D_MODEL), lambda i: (i, 0)),
                   pl.BlockSpec((N_EXPERTS, tm), lambda i: (0, i))],
        out_shape=[jax.ShapeDtypeStruct((T_ALL, D_MODEL), F32),
                   jax.ShapeDtypeStruct((T_ALL, D_MODEL), F32),
                   jax.ShapeDtypeStruct((N_EXPERTS, T_ALL), F32)],
        compiler_params=_params(("arbitrary",)),
        name=f"merge{layer}",
    )(oa_c, op_c, oa_s, op_s, proj, proj, x, mods, w_ba, w_bp, w_out, norm2_g.reshape(DEPTH, 1, D_MODEL), router_wt)


def _route_kernel(lg_ref, rb_ref, tri_ref, idx_ref, w_ref, rank_ref, cnt_ref, carry_ref):
    @pl.when(pl.program_id(0) == 0)
    def _():
        carry_ref[...] = jnp.zeros_like(carry_ref)

    scores = jax.nn.sigmoid(lg_ref[...])
    sel = scores + rb_ref[...]
    gscore = []
    for g in range(N_GROUPS):
        a, b, c, d = [sel[EXPERTS_PER_GROUP * g + i:EXPERTS_PER_GROUP * g + i + 1, :] for i in range(4)]
        hi1, lo1, hi2, lo2 = jnp.maximum(a, b), jnp.minimum(a, b), jnp.maximum(c, d), jnp.minimum(c, d)
        gscore.append(jnp.maximum(hi1, hi2) + jnp.maximum(jnp.minimum(hi1, hi2), jnp.maximum(lo1, lo2)))
    best = jnp.zeros_like(gscore[0], dtype=jnp.int32)
    bestv = gscore[0]
    for g in range(1, N_GROUPS):
        upd = gscore[g] > bestv
        best = jnp.where(upd, g, best)
        bestv = jnp.where(upd, gscore[g], bestv)
    row = lax.broadcasted_iota(jnp.int32, sel.shape, 0)
    masked = jnp.where(row // EXPERTS_PER_GROUP == best, sel, -jnp.inf)
    i0 = jnp.min(jnp.where(masked == masked.max(axis=0, keepdims=True), row, N_EXPERTS), axis=0, keepdims=True)
    masked = jnp.where(row == i0, -jnp.inf, masked)
    i1 = jnp.min(jnp.where(masked == masked.max(axis=0, keepdims=True), row, N_EXPERTS), axis=0, keepdims=True)
    s0 = jnp.sum(jnp.where(row == i0, scores, 0.0), axis=0, keepdims=True)
    s1 = jnp.sum(jnp.where(row == i1, scores, 0.0), axis=0, keepdims=True)
    den = s0 + s1
    idx_ref[...] = jnp.concatenate([i0, i1], axis=0)
    w_ref[...] = jnp.concatenate([s0 / den, s1 / den], axis=0)

    hit = jnp.where((row == i0) | (row == i1), 1.0, 0.0)
    before = _dot(hit.astype(BF16), tri_ref[...]) + carry_ref[:, 0:1]
    r0 = jnp.sum(jnp.where(row == i0, before, 0.0), axis=0, keepdims=True)
    r1 = jnp.sum(jnp.where(row == i1, before, 0.0), axis=0, keepdims=True)
    rank_ref[...] = jnp.concatenate([r0, r1], axis=0).astype(jnp.int32)
    carry_ref[...] = carry_ref[...] + jnp.sum(hit, axis=1, keepdims=True)
    cnt_ref[...] = carry_ref[...]


def _route(logits_t, router_bias):
    tn = ROUTE_BLOCK
    tri = jnp.asarray(np.triu(np.ones((tn, tn), np.float32), k=1), BF16)
    pair = lambda dt: jax.ShapeDtypeStruct((TOP_K, T_ALL), dt)
    return pl.pallas_call(
        _route_kernel,
        grid=(T_ALL // tn,),
        in_specs=[pl.BlockSpec((N_EXPERTS, tn), lambda i: (0, i)),
                  pl.BlockSpec((N_EXPERTS, 1), lambda i: (0, 0)),
                  pl.BlockSpec((tn, tn), lambda i: (0, 0))],
        out_specs=[pl.BlockSpec((TOP_K, tn), lambda i: (0, i)),
                   pl.BlockSpec((TOP_K, tn), lambda i: (0, i)),
                   pl.BlockSpec((TOP_K, tn), lambda i: (0, i)),
                   pl.BlockSpec((N_EXPERTS, LANES), lambda i: (0, 0))],
        out_shape=[pair(jnp.int32), pair(F32), pair(jnp.int32),
                   jax.ShapeDtypeStruct((N_EXPERTS, LANES), F32)],
        scratch_shapes=[pltpu.VMEM((N_EXPERTS, LANES), F32)],
        compiler_params=_params(("arbitrary",)),
        name="route",
    )(logits_t, router_bias.reshape(N_EXPERTS, 1).astype(F32), tri)


def _slot_kernel(idx_ref, rank_ref, off_ref, pos_ref):
    row = lax.broadcasted_iota(jnp.int32, (N_EXPERTS, idx_ref.shape[1]), 0)
    for j in range(TOP_K):
        off = jnp.sum(jnp.where(row == idx_ref[j:j + 1, :], off_ref[...], 0), axis=0, keepdims=True)
        pos_ref[j:j + 1, :] = off + rank_ref[j:j + 1, :]


def _slots(idx_t, rank_t, seg_start):
    tn = ROUTE_BLOCK
    blk = pl.BlockSpec((TOP_K, tn), lambda i: (0, i))
    return pl.pallas_call(
        _slot_kernel,
        grid=(T_ALL // tn,),
        in_specs=[blk, blk, pl.BlockSpec((N_EXPERTS, 1), lambda i: (0, 0))],
        out_specs=blk,
        out_shape=jax.ShapeDtypeStruct((TOP_K, T_ALL), jnp.int32),
        compiler_params=_params(("arbitrary",)),
        name="slots",
    )(idx_t, rank_t, seg_start.reshape(N_EXPERTS, 1))


def _tile_plan(counts):
    cnt = counts.astype(jnp.int32)
    ntile = (cnt + MOE_TM - 1) // MOE_TM
    tile_end = jnp.cumsum(ntile)
    seg_start = (tile_end - ntile) * MOE_TM
    n_used = tile_end[-1]
    tiles = jnp.arange(MOE_TILES, dtype=jnp.int32)
    live = jnp.minimum(tiles, n_used - 1)
    expert = jnp.sum((live[:, None] >= tile_end[None, :]).astype(jnp.int32), axis=1)
    first = ((live == (tile_end - ntile)[expert]) & (tiles < n_used)).astype(jnp.int32)
    return seg_start, expert, first, n_used.reshape(1)


def _row_copy(src, i, dst, j, sem):
    return pltpu.make_async_copy(src.at[pl.ds(i, 1), :], dst.at[pl.ds(j, 1), :], sem)


def _invert_kernel(pos0_ref, pos1_ref, src_ref):
    def fill(s, carry):
        src_ref[s] = 0
        return carry

    def put(t, carry):
        src_ref[pos0_ref[t]] = t
        src_ref[pos1_ref[t]] = t
        return carry

    lax.fori_loop(0, MOE_TILES * MOE_TM, fill, 0, unroll=8)
    lax.fori_loop(0, T_ALL, put, 0, unroll=8)


def _invert(pos):
    return pl.pallas_call(
        _invert_kernel,
        grid_spec=pltpu.PrefetchScalarGridSpec(
            num_scalar_prefetch=2,
            grid=(1,),
            in_specs=[],
            out_specs=pl.BlockSpec(memory_space=pltpu.SMEM),
        ),
        out_shape=jax.ShapeDtypeStruct((MOE_TILES * MOE_TM,), jnp.int32),
        compiler_params=_params(("arbitrary",)),
        name="invert",
    )(pos[0], pos[1])


def _moe_kernel(src_ref, expert_ref, first_ref, nused_ref, h_ref, wg_ref, wu_ref, wd_ref, y_ref,
                xbuf, sems, wg_bf, wu_bf, wd_bf):
    i = pl.program_id(0)
    n_used = nused_ref[0]
    slot = i % 2

    def issue(tile, slot):
        def body(r, carry):
            _row_copy(h_ref, src_ref[tile * MOE_TM + r], xbuf.at[slot], r, sems.at[slot]).start()
            return carry
        lax.fori_loop(0, MOE_TM, body, 0, unroll=8)

    @pl.when(i == 0)
    def _():
        issue(0, 0)

    @pl.when(i + 1 < n_used)
    def _():
        issue(i + 1, 1 - slot)

    @pl.when(first_ref[i] == 1)
    def _():
        wg_bf[...] = wg_ref[0, 0].astype(BF16)
        wu_bf[...] = wu_ref[0, 0].astype(BF16)
        wd_bf[...] = wd_ref[0, 0].astype(BF16)

    @pl.when(i < n_used)
    def _():
        for _ in range(MOE_TM):
            _row_copy(h_ref, 0, xbuf.at[slot], 0, sems.at[slot]).wait()
        x = xbuf[slot].astype(BF16)
        act = jax.nn.silu(_dot(x, wg_bf[...])) * _dot(x, wu_bf[...])
        y_ref[...] = _dot(act.astype(BF16), wd_bf[...])

    @pl.when(i >= n_used)
    def _():
        y_ref[...] = jnp.zeros_like(y_ref)


def _moe(h2, src, expert, first, n_used, w_gate, w_up, w_down, layer):
    wspec = lambda shape: pl.BlockSpec((1, 1) + shape, lambda i, sr, ex, fi, nu: (layer, ex[i], 0, 0))
    return pl.pallas_call(
        _moe_kernel,
        grid_spec=pltpu.PrefetchScalarGridSpec(
            num_scalar_prefetch=4,
            grid=(MOE_TILES,),
            in_specs=[pl.BlockSpec(memory_space=pl.ANY),
                      wspec((D_MODEL, EXPERT_FF)), wspec((D_MODEL, EXPERT_FF)), wspec((EXPERT_FF, D_MODEL))],
            out_specs=pl.BlockSpec((MOE_TM, D_MODEL), lambda i, sr, ex, fi, nu: (i, 0)),
            scratch_shapes=[pltpu.VMEM((2, MOE_TM, D_MODEL), F32), pltpu.SemaphoreType.DMA((2,)),
                            pltpu.VMEM((D_MODEL, EXPERT_FF), BF16), pltpu.VMEM((D_MODEL, EXPERT_FF), BF16),
                            pltpu.VMEM((EXPERT_FF, D_MODEL), BF16)],
        ),
        out_shape=jax.ShapeDtypeStruct((MOE_TILES * MOE_TM, D_MODEL), F32),
        compiler_params=_params(("arbitrary",)),
        name=f"moe{layer}",
    )(src, expert, first, n_used, h2, w_gate, w_up, w_down)


def _combine_kernel(pos0_ref, pos1_ref, ys_ref, x_ref, w_ref, mod_ref, o_ref, ybuf, sems):
    i = pl.program_id(0)
    n = pl.num_programs(0)
    tm = x_ref.shape[0]

    def issue(tile, slot):
        def body(r, carry):
            t = tile * tm + r
            _row_copy(ys_ref, pos0_ref[t], ybuf.at[slot, 0], r, sems.at[slot]).start()
            _row_copy(ys_ref, pos1_ref[t], ybuf.at[slot, 1], r, sems.at[slot]).start()
            return carry
        lax.fori_loop(0, tm, body, 0)

    slot = i % 2

    @pl.when(i == 0)
    def _():
        issue(0, 0)

    @pl.when(i + 1 < n)
    def _():
        issue(i + 1, 1 - slot)

    for _ in range(TOP_K * tm):
        _row_copy(ys_ref, 0, ybuf.at[slot, 0], 0, sems.at[slot]).wait()

    w0 = _row_to_col(w_ref[0:1, :])
    w1 = _row_to_col(w_ref[1:2, :])
    y = w0 * ybuf[slot, 0] + w1 * ybuf[slot, 1]
    o_ref[...] = x_ref[...] + mod_ref[0, 0, G2:G2 + 1, :] * y


def _combine(ys, pos, w_t, x, mods, layer):
    tm = COMBINE_TM
    return pl.pallas_call(
        _combine_kernel,
        grid_spec=pltpu.PrefetchScalarGridSpec(
            num_scalar_prefetch=2,
            grid=(T_ALL // tm,),
            in_specs=[pl.BlockSpec(memory_space=pl.ANY),
                      pl.BlockSpec((tm, D_MODEL), lambda i, p0, p1: (i, 0)),
                      pl.BlockSpec((TOP_K, tm), lambda i, p0, p1: (0, i)),
                      pl.BlockSpec((1, 1, N_MOD, D_MODEL), lambda i, p0, p1: (layer, _cond_of_tile(i, tm), 0, 0))],
            out_specs=pl.BlockSpec((tm, D_MODEL), lambda i, p0, p1: (i, 0)),
            scratch_shapes=[pltpu.VMEM((2, TOP_K, tm, D_MODEL), F32), pltpu.SemaphoreType.DMA((2,))],
        ),
        out_shape=jax.ShapeDtypeStruct((T_ALL, D_MODEL), F32),
        compiler_params=_params(("arbitrary",)),
        name=f"combine{layer}",
    )(pos[0], pos[1], ys, x, w_t, mods)


def _final_kernel(x_ref, g_ref, o_ref):
    o_ref[...] = _rms(x_ref[...]) * g_ref[...]


def _final(x, final_g):
    tm = 512
    return pl.pallas_call(
        _final_kernel,
        grid=(T_ALL // tm,),
        in_specs=[pl.BlockSpec((tm, D_MODEL), lambda i: (i, 0)),
                  pl.BlockSpec((1, D_MODEL), lambda i: (0, 0))],
        out_specs=pl.BlockSpec((tm, D_MODEL), lambda i: (i, 0)),
        out_shape=jax.ShapeDtypeStruct((T_ALL, D_MODEL), F32),
        compiler_params=_params(("arbitrary",)),
        name="final_norm",
    )(x, final_g.reshape(1, D_MODEL))


def kernel(x_prompt, x_sample, c, cache_k, cache_v, c_ctx, ada_w, ada_b, norm1_g, w_in, rpb, pool_w, pool_scale,
           w_branch_a, w_branch_p, w_out, norm2_g, router_w, router_bias, moe_w_gate, moe_w_up, moe_w_down, final_g):
    x = jnp.concatenate([x_prompt.reshape(T_CTX, D_MODEL), x_sample.reshape(T_SMP, D_MODEL)], axis=0)
    cond = jnp.concatenate([c_ctx[None, :], c, jnp.zeros((N_COND - 1 - DEC_BATCH, D_MODEL), F32)], axis=0)
    mods = _ada(cond, ada_w, ada_b).reshape(DEPTH, N_COND, N_MOD, D_MODEL)
    rpb_pad = jnp.pad(rpb.astype(F32), ((0, 0), (0, 0), (0, 1), (0, LANES - rpb.shape[-1])))
    w_ba, w_bp, w_o = w_branch_a.astype(BF16), w_branch_p.astype(BF16), w_out.astype(BF16)
    router_wt = router_w.T

    new_ks, new_vs = [], []
    for l in range(DEPTH):
        proj = _inproj(x, norm1_g, mods, w_in, l)
        new_ks.append(proj[:T_CTX, NA_WIDTH:2 * NA_WIDTH].reshape(BATCH, SEQ, NA_HEADS, NA_HEAD_DIM))
        new_vs.append(proj[:T_CTX, 2 * NA_WIDTH:3 * NA_WIDTH].reshape(BATCH, SEQ, NA_HEADS, NA_HEAD_DIM))
        oa_c, op_c = _ctx_mixer(proj, pool_w, pool_scale, l)
        oa_s, op_s = _na_mixer(proj, cache_k, cache_v, rpb_pad, pool_w, pool_scale, l)
        x, h2, logits_t = _merge(oa_c, op_c, oa_s, op_s, proj, x, mods, w_ba, w_bp, w_o, norm2_g, router_wt, l)
        idx_t, w_t, rank_t, counts = _route(logits_t, router_bias)
        seg_start, expert, first, n_used = _tile_plan(counts[:, 0])
        pos = _slots(idx_t, rank_t, seg_start)
        ys = _moe(h2, _invert(pos), expert, first, n_used, moe_w_gate, moe_w_up, moe_w_down, l)
        x = _combine(ys, pos, w_t, x, mods, l)
    y = _final(x, final_g)
    y_prompt = y[:T_CTX].reshape(BATCH, SEQ, D_MODEL)
    y_sample = y[T_CTX:].reshape(DEC_BATCH, DEC_SEQ, D_MODEL)
    return (y_prompt, y_sample, jnp.stack(new_ks, axis=1), jnp.stack(new_vs, axis=1))
```

```python
import functools

import numpy as np
import jax
import jax.numpy as jnp
from jax import lax
from jax.experimental import pallas as pl
from jax.experimental.pallas import tpu as pltpu

D_MODEL = 2048
BATCH = 32
SEQ = 256
DEPTH = 2
DEC_BATCH = 2
DEC_SEQ = 2048
PAST_LEN = 256
GRID_W = 64
GRID_ROWS = DEC_SEQ // GRID_W
NA_HEADS = 16
NA_HEAD_DIM = 64
NA_WIDTH = NA_HEADS * NA_HEAD_DIM
WIN_H = 8
WIN_W = 16
POOL_WIDTH = D_MODEL // 2
POOL_WINDOWS = (2, 4, 8, 16)
POOL_GROUPS = len(POOL_WINDOWS)
POOL_GC = POOL_WIDTH // POOL_GROUPS
IN_WIDTH = 3 * NA_WIDTH + POOL_WIDTH + 2 * D_MODEL
N_EXPERTS = 16
N_GROUPS = 4
EXPERTS_PER_GROUP = N_EXPERTS // N_GROUPS
TOP_K = 2
EXPERT_FF = D_MODEL // 4
N_MOD = 6
EPS = 1e-6

T_CTX = BATCH * SEQ
T_SMP = DEC_BATCH * DEC_SEQ
T_ALL = T_CTX + T_SMP
N_COND = 8
NA_KEYS = WIN_H * GRID_W
MASKED = -1e30

F32 = jnp.float32
BF16 = jnp.bfloat16
VMEM_LIMIT = 56 * 1024 * 1024

SH1, SC1, G1, SH2, SC2, G2 = range(6)
LANES = 128
PAIR = 2 * NA_HEAD_DIM
ROUTE_BLOCK = 512
MOE_TM = 256
MOE_TILES = T_ALL * TOP_K // MOE_TM + N_EXPERTS
COMBINE_TM = 256


def _params(sem, vmem=VMEM_LIMIT, **kw):
    return pltpu.CompilerParams(dimension_semantics=sem, vmem_limit_bytes=vmem, **kw)


def _cond_of_tile(i, tm):
    t0 = i * tm
    return jnp.where(t0 < T_CTX, 0, 1 + (t0 - T_CTX) // DEC_SEQ)


def _mod_spec(layer, tm):
    return pl.BlockSpec((1, 1, N_MOD, D_MODEL), lambda i, *_: (layer, _cond_of_tile(i, tm), 0, 0))


def _two_group_specs(tm, width):
    nc = T_CTX // tm
    return [pl.BlockSpec((tm, width), lambda i, *_: (jnp.minimum(i, nc - 1), 0)),
            pl.BlockSpec((tm, width), lambda i, *_: (jnp.maximum(i - nc, 0), 0))]


def _rms(x):
    return x * lax.rsqrt(jnp.mean(x * x, axis=-1, keepdims=True) + EPS)


def _mod_norm(x, g, mod_ref, shift_row, scale_row):
    return (_rms(x) * g * (1.0 + mod_ref[0, 0, scale_row:scale_row + 1, :])
            + mod_ref[0, 0, shift_row:shift_row + 1, :])


def _dot(a, b):
    return jnp.dot(a, b, preferred_element_type=F32)


def _dot_nt(a, b):
    return lax.dot_general(a, b, (((1,), (1,)), ((), ())), preferred_element_type=F32)


def _split2(x):
    x1 = x.astype(BF16)
    return x1, (x - x1.astype(F32)).astype(BF16)


def _row_to_col(v):
    n = v.shape[1]
    r = lax.broadcasted_iota(jnp.int32, (n, n), 0)
    c = lax.broadcasted_iota(jnp.int32, (n, n), 1)
    return jnp.sum(jnp.where(r == c, v, 0.0), axis=1, keepdims=True)


def _ada_kernel(c_ref, w_ref, b_ref, o_ref):
    s = jax.nn.silu(c_ref[...]).astype(BF16)
    o_ref[0] = _dot(s, w_ref[0].astype(BF16)) + b_ref[0]


def _ada(cond, ada_w, ada_b):
    tn = 1024
    nj = N_MOD * D_MODEL // tn
    return pl.pallas_call(
        _ada_kernel,
        grid=(DEPTH, nj),
        in_specs=[
            pl.BlockSpec((N_COND, D_MODEL), lambda l, j: (0, 0)),
            pl.BlockSpec((1, D_MODEL, tn), lambda l, j: (l, 0, j)),
            pl.BlockSpec((1, 1, tn), lambda l, j: (l, 0, j)),
        ],
        out_specs=pl.BlockSpec((1, N_COND, tn), lambda l, j: (l, 0, j)),
        out_shape=jax.ShapeDtypeStruct((DEPTH, N_COND, N_MOD * D_MODEL), F32),
        compiler_params=_params(("arbitrary", "arbitrary")),
        name="ada",
    )(cond, ada_w, ada_b.reshape(DEPTH, 1, N_MOD * D_MODEL))


def _pre_kernel(xc_ref, xs_ref, g_ref, mod_ref, h_ref):
    is_ctx = pl.program_id(0) < T_CTX // h_ref.shape[0]
    x = jnp.where(is_ctx, xc_ref[...], xs_ref[...])
    h_ref[...] = _mod_norm(x, g_ref[0], mod_ref, SH1, SC1).astype(BF16)


def _pre(x_ctx, x_smp, norm_g, mods):
    tm = 512
    return pl.pallas_call(
        _pre_kernel,
        grid=(T_ALL // tm,),
        in_specs=_two_group_specs(tm, D_MODEL) + [
            pl.BlockSpec((1, 1, D_MODEL), lambda i: (0, 0, 0)),
            _mod_spec(0, tm),
        ],
        out_specs=pl.BlockSpec((tm, D_MODEL), lambda i: (i, 0)),
        out_shape=jax.ShapeDtypeStruct((T_ALL, D_MODEL), BF16),
        compiler_params=_params(("arbitrary",)),
        name="pre",
    )(x_ctx, x_smp, norm_g.reshape(DEPTH, 1, D_MODEL), mods)


INPROJ_TM = 1024
INPROJ_TN = 512
_NI = T_ALL // INPROJ_TM
_NC = T_CTX // INPROJ_TM
_COLS = {name: (start // INPROJ_TN, width // INPROJ_TN) for name, start, width in (
    ("q", 0, NA_WIDTH), ("k", NA_WIDTH, NA_WIDTH), ("v", 2 * NA_WIDTH, NA_WIDTH),
    ("u", 3 * NA_WIDTH, POOL_WIDTH), ("g", 3 * NA_WIDTH + POOL_WIDTH, 2 * D_MODEL))}


def _visit(name, rows, last_row):
    j0, nj = _COLS[name]

    def index(j, i):
        row = jnp.where(j < j0, 0, jnp.where(j < j0 + nj, rows(i), last_row))
        return row, jnp.clip(j - j0, 0, nj - 1)
    return index


def _inproj_kernel(h_ref, w_ref, q_ref, kc_ref, ks_ref, vc_ref, vs_ref, u_ref, g_ref, wbf_ref):
    j = pl.program_id(0)
    i = pl.program_id(1)

    @pl.when(i == 0)
    def _():
        wbf_ref[...] = w_ref[0].astype(BF16)

    def project():
        return _dot(h_ref[...], wbf_ref[...])

    def during(name):
        j0, nj = _COLS[name]
        return (j >= j0) & (j < j0 + nj)

    @pl.when(during("q"))
    def _():
        q_ref[...] = (project() * NA_HEAD_DIM ** -0.5).astype(BF16)

    for name, ctx_ref, smp_ref in (("k", kc_ref, ks_ref), ("v", vc_ref, vs_ref)):
        @pl.when(during(name) & (i < _NC))
        def _():
            ctx_ref[...] = project()

        @pl.when(during(name) & (i >= _NC))
        def _():
            smp_ref[...] = project().astype(BF16)

    @pl.when(during("u"))
    def _():
        u_ref[...] = project()

    @pl.when(during("g"))
    def _():
        g_ref[...] = project()


def _inproj(h, w_in, layer):
    tm, tn = INPROJ_TM, INPROJ_TN
    blk = pl.BlockSpec
    every = lambda i: i
    ctx_rows = lambda i: jnp.minimum(i, _NC - 1)
    smp_rows = lambda i: jnp.clip(i - _NC, 0, _NI - _NC - 1)
    sds = jax.ShapeDtypeStruct
    return pl.pallas_call(
        _inproj_kernel,
        grid=(IN_WIDTH // tn, _NI),
        in_specs=[blk((tm, D_MODEL), lambda j, i: (i, 0)),
                  blk((1, D_MODEL, tn), lambda j, i: (layer, 0, j))],
        out_specs=[blk((tm, tn), _visit("q", every, _NI - 1)),
                   blk((tm, tn), _visit("k", ctx_rows, _NC - 1)),
                   blk((tm, tn), _visit("k", smp_rows, _NI - _NC - 1)),
                   blk((tm, tn), _visit("v", ctx_rows, _NC - 1)),
                   blk((tm, tn), _visit("v", smp_rows, _NI - _NC - 1)),
                   blk((tm, tn), _visit("u", every, _NI - 1)),
                   blk((tm, tn), _visit("g", every, _NI - 1))],
        out_shape=[sds((T_ALL, NA_WIDTH), BF16),
                   sds((T_CTX, NA_WIDTH), F32), sds((T_SMP, NA_WIDTH), BF16),
                   sds((T_CTX, NA_WIDTH), F32), sds((T_SMP, NA_WIDTH), BF16),
                   sds((T_ALL, POOL_WIDTH), F32),
                   sds((T_ALL, 2 * D_MODEL), F32)],
        scratch_shapes=[pltpu.VMEM((D_MODEL, tn), BF16)],
        compiler_params=_params(("arbitrary", "arbitrary")),
        name=f"inproj{layer}",
    )(h, w_in)


def _pool_consts(n, seg):
    t = np.arange(n)
    pos = t % seg
    base = t - pos
    mats = np.zeros((POOL_GROUPS, n, n), np.float32)
    cnts = np.zeros((POOL_GROUPS, n, 1), np.float32)
    for gi, w in enumerate(POOL_WINDOWS):
        lo = np.clip(pos - w // 2, 0, seg)
        hi = np.clip(pos + w - w // 2, 0, seg)
        s = t[None, :]
        mats[gi] = ((s >= (base + lo)[:, None]) & (s < (base + hi)[:, None])).astype(np.float32)
        cnts[gi, :, 0] = hi - lo
    return jnp.asarray(mats, BF16), jnp.asarray(cnts, F32)


def _pool(u_ref, band_ref, cnt_ref, pw_ref, ps_ref, op_ref):
    for gi in range(POOL_GROUPS):
        cols = slice(gi * POOL_GC, (gi + 1) * POOL_GC)
        ug = u_ref[:, cols]
        u1 = ug.astype(BF16)
        u2 = (ug - u1.astype(F32)).astype(BF16)
        u3 = (ug - u1.astype(F32) - u2.astype(F32)).astype(BF16)
        band = band_ref[gi]
        wsum = _dot(band, u1) + _dot(band, u2) + _dot(band, u3)
        d = (wsum / cnt_ref[gi] - ug).astype(BF16)
        y = _dot(d, pw_ref[0, gi].astype(BF16))
        op_ref[:, cols] = (y * ps_ref[0, :, cols]).astype(BF16)


def _softmax_av(s_parts, v_parts):
    m = s_parts[0].max(axis=-1, keepdims=True)
    for s in s_parts[1:]:
        m = jnp.maximum(m, s.max(axis=-1, keepdims=True))
    den = None
    acc = None
    for s, v in zip(s_parts, v_parts):
        p = jnp.exp(s - m)
        ps = p.sum(axis=-1, keepdims=True)
        den = ps if den is None else den + ps
        pv = _dot(p.astype(BF16), v)
        acc = pv if acc is None else acc + pv
    return acc / den


def _attend_pair(q, ks, vs, biases):
    m = q.shape[0]
    first = lax.broadcasted_iota(jnp.int32, (m, PAIR), 1) < NA_HEAD_DIM
    qf = q.astype(F32)
    q2 = jnp.concatenate([jnp.where(first, qf, 0.0), jnp.where(first, 0.0, qf)], axis=0).astype(BF16)
    scores = [_dot_nt(q2, k) if b is None else _dot_nt(q2, k) + b for k, b in zip(ks, biases)]
    o2 = _softmax_av(scores, vs)
    return jnp.where(first, o2[:m], o2[m:])


def _ctx_mixer_kernel(q_ref, k_ref, v_ref, u_ref, band_ref, cnt_ref, pw_ref, ps_ref, oa_ref, op_ref):
    for hp in range(NA_HEADS // 2):
        cols = slice(hp * PAIR, (hp + 1) * PAIR)
        o = _attend_pair(q_ref[:, cols], [k_ref[:, cols].astype(BF16)], [v_ref[:, cols].astype(BF16)], [None])
        oa_ref[:, cols] = o.astype(BF16)
    _pool(u_ref, band_ref, cnt_ref, pw_ref, ps_ref, op_ref)


def _ctx_mixer(q, kc, vc, u, pool_w, pool_scale, layer):
    band, cnt = _pool_consts(SEQ, SEQ)
    seq = lambda width: pl.BlockSpec((SEQ, width), lambda s: (s, 0))
    return pl.pallas_call(
        _ctx_mixer_kernel,
        grid=(BATCH,),
        in_specs=[
            seq(NA_WIDTH), seq(NA_WIDTH), seq(NA_WIDTH), seq(POOL_WIDTH),
            pl.BlockSpec((POOL_GROUPS, SEQ, SEQ), lambda s: (0, 0, 0)),
            pl.BlockSpec((POOL_GROUPS, SEQ, 1), lambda s: (0, 0, 0)),
            pl.BlockSpec((1, POOL_GROUPS, POOL_GC, POOL_GC), lambda s: (layer, 0, 0, 0)),
            pl.BlockSpec((1, 1, POOL_WIDTH), lambda s: (layer, 0, 0)),
        ],
        out_specs=[seq(NA_WIDTH), seq(POOL_WIDTH)],
        out_shape=[jax.ShapeDtypeStruct((T_CTX, NA_WIDTH), BF16),
                   jax.ShapeDtypeStruct((T_CTX, POOL_WIDTH), BF16)],
        compiler_params=_params(("arbitrary",)),
        name=f"ctx_mixer{layer}",
    )(q, kc, vc, u, band, cnt, pool_w, pool_scale.reshape(DEPTH, 1, POOL_WIDTH))


def _na_build_bias(rpb_ref, bias_ref, r, rs):
    lanes = 2 * GRID_W
    qc = lax.broadcasted_iota(jnp.int32, (GRID_W, lanes), 0)
    lane = lax.broadcasted_iota(jnp.int32, (GRID_W, lanes), 1)
    kc = jnp.bitwise_and(lane, GRID_W - 1)
    cstart = jnp.clip(qc - WIN_W // 2, 0, GRID_W - WIN_W)
    valid = (kc >= cstart) & (kc < cstart + WIN_W)
    for h in range(NA_HEADS):
        for jp in range(WIN_H // 2):
            halves = []
            for j, shift in ((2 * jp, lanes - (WIN_W - 1)), (2 * jp + 1, GRID_W - (WIN_W - 1))):
                dr = rs + j - r + (WIN_H - 1)
                row = jnp.broadcast_to(rpb_ref[0, h, pl.ds(dr, 1), :], (GRID_W, lanes))
                halves.append(pltpu.roll(row, shift, 1, stride=1, stride_axis=0))
            tile = jnp.where(lane < GRID_W, halves[0], halves[1])
            bias_ref[h, :, jp * lanes:(jp + 1) * lanes] = jnp.where(valid, tile, MASKED)


def _na_mixer_kernel(q_ref, k_ref, v_ref, ck_ref, cv_ref, rpb_ref, u_ref, band_ref, cnt_ref, pw_ref, ps_ref,
                     oa_ref, op_ref, bias_ref, ckb_ref, cvb_ref):
    r = pl.program_id(1)
    rs = jnp.clip(r - WIN_H // 2, 0, GRID_ROWS - WIN_H)

    @pl.when(r == 0)
    def _():
        ckb_ref[...] = ck_ref[0, 0].astype(BF16)
        cvb_ref[...] = cv_ref[0, 0].astype(BF16)

    @pl.when((r <= WIN_H // 2) | (r > GRID_ROWS - WIN_H // 2))
    def _():
        _na_build_bias(rpb_ref, bias_ref, r, rs)

    keys = pl.ds(pl.multiple_of(rs * GRID_W, GRID_W), NA_KEYS)
    for hp in range(NA_HEADS // 2):
        cols = slice(hp * PAIR, (hp + 1) * PAIR)
        bias = bias_ref[2 * hp:2 * hp + 2].reshape(2 * GRID_W, NA_KEYS)
        o = _attend_pair(q_ref[:, cols], [k_ref[keys, cols], ckb_ref[:, cols]],
                         [v_ref[keys, cols], cvb_ref[:, cols]], [bias, None])
        oa_ref[:, cols] = o.astype(BF16)
    _pool(u_ref, band_ref, cnt_ref, pw_ref, ps_ref, op_ref)


def _na_mixer(q, ks, vs, u, cache_k, cache_v, rpb_pad, pool_w, pool_scale, layer):
    band, cnt = _pool_consts(GRID_W, GRID_W)
    row0 = T_CTX // GRID_W
    row = lambda width: pl.BlockSpec((GRID_W, width), lambda b, r: (row0 + b * GRID_ROWS + r, 0))
    batch = pl.BlockSpec((DEC_SEQ, NA_WIDTH), lambda b, r: (b, 0))
    cblk = pl.BlockSpec((1, 1, PAST_LEN, NA_WIDTH), lambda b, r: (b, layer, 0, 0))
    out = lambda width: pl.BlockSpec((GRID_W, width), lambda b, r: (b * GRID_ROWS + r, 0))
    return pl.pallas_call(
        _na_mixer_kernel,
        grid=(DEC_BATCH, GRID_ROWS),
        in_specs=[
            row(NA_WIDTH), batch, batch, cblk, cblk,
            pl.BlockSpec((1,) + rpb_pad.shape[1:], lambda b, r: (layer, 0, 0, 0)),
            row(POOL_WIDTH),
            pl.BlockSpec((POOL_GROUPS, GRID_W, GRID_W), lambda b, r: (0, 0, 0)),
            pl.BlockSpec((POOL_GROUPS, GRID_W, 1), lambda b, r: (0, 0, 0)),
            pl.BlockSpec((1, POOL_GROUPS, POOL_GC, POOL_GC), lambda b, r: (layer, 0, 0, 0)),
            pl.BlockSpec((1, 1, POOL_WIDTH), lambda b, r: (layer, 0, 0)),
        ],
        out_specs=[out(NA_WIDTH), out(POOL_WIDTH)],
        out_shape=[jax.ShapeDtypeStruct((T_SMP, NA_WIDTH), BF16),
                   jax.ShapeDtypeStruct((T_SMP, POOL_WIDTH), BF16)],
        scratch_shapes=[pltpu.VMEM((NA_HEADS, GRID_W, NA_KEYS), F32),
                        pltpu.VMEM((PAST_LEN, NA_WIDTH), BF16), pltpu.VMEM((PAST_LEN, NA_WIDTH), BF16)],
        compiler_params=_params(("arbitrary", "arbitrary")),
        name=f"na_mixer{layer}",
    )(q, ks, vs,
      cache_k.reshape(DEC_BATCH, DEPTH, PAST_LEN, NA_WIDTH), cache_v.reshape(DEC_BATCH, DEPTH, PAST_LEN, NA_WIDTH),
      rpb_pad, u, band, cnt, pool_w, pool_scale.reshape(DEPTH, 1, POOL_WIDTH))


def _merge_kernel(n_x, oac_ref, opc_ref, oas_ref, ops_ref, ga_ref, gp_ref, *refs):
    x_refs, (mod_ref, wba_ref, wbp_ref, wout_ref, n2g_ref, rwt_ref, xo_ref, h2_ref, lg_ref) = refs[:n_x], refs[n_x:]
    is_ctx = pl.program_id(0) < T_CTX // xo_ref.shape[0]
    a = _dot(jnp.where(is_ctx, oac_ref[...], oas_ref[...]), wba_ref[0])
    p = _dot(jnp.where(is_ctx, opc_ref[...], ops_ref[...]), wbp_ref[0])
    mix = jax.nn.sigmoid(ga_ref[...]) * a + jax.nn.sigmoid(gp_ref[...]) * p
    m = _dot(mix.astype(BF16), wout_ref[0])
    x_in = x_refs[0][...] if n_x == 1 else jnp.where(is_ctx, x_refs[0][...], x_refs[1][...])
    x = x_in + mod_ref[0, 0, G1:G1 + 1, :] * m
    xo_ref[...] = x
    h2 = _mod_norm(x, n2g_ref[0], mod_ref, SH2, SC2)
    h2_ref[...] = h2
    w1, w2 = _split2(rwt_ref[...])
    t1, t2 = _split2(h2)
    lg_ref[...] = _dot_nt(w1, t1) + _dot_nt(w1, t2) + _dot_nt(w2, t1)


def _merge(oa_c, op_c, oa_s, op_s, gates, x_parts, mods, w_ba, w_bp, w_out, norm2_g, router_wt, layer):
    tm = 256
    full = lambda shape: pl.BlockSpec(shape, lambda i: (layer,) + (0,) * (len(shape) - 1))
    x_specs = ([pl.BlockSpec((tm, D_MODEL), lambda i: (i, 0))] if len(x_parts) == 1
               else _two_group_specs(tm, D_MODEL))
    return pl.pallas_call(
        functools.partial(_merge_kernel, len(x_parts)),
        grid=(T_ALL // tm,),
        in_specs=_two_group_specs(tm, NA_WIDTH)[:1] + _two_group_specs(tm, POOL_WIDTH)[:1]
        + _two_group_specs(tm, NA_WIDTH)[1:] + _two_group_specs(tm, POOL_WIDTH)[1:] + [
            pl.BlockSpec((tm, D_MODEL), lambda i: (i, 0)),
            pl.BlockSpec((tm, D_MODEL), lambda i: (i, 1)),
        ] + x_specs + [
            _mod_spec(layer, tm),
            full((1, NA_WIDTH, D_MODEL)),
            full((1, POOL_WIDTH, D_MODEL)),
            full((1, D_MODEL, D_MODEL)),
            full((1, 1, D_MODEL)),
            pl.BlockSpec((N_EXPERTS, D_MODEL), lambda i: (0, 0)),
        ],
        out_specs=[pl.BlockSpec((tm, D_MODEL), lambda i: (i, 0)),
                   pl.BlockSpec((tm, D_MODEL), lambda i: (i, 0)),
                   pl.BlockSpec((N_EXPERTS, tm), lambda i: (0, i))],
        out_shape=[jax.ShapeDtypeStruct((T_ALL, D_MODEL), F32),
                   jax.ShapeDtypeStruct((T_ALL, D_MODEL), F32),
                   jax.ShapeDtypeStruct((N_EXPERTS, T_ALL), F32)],
        compiler_params=_params(("arbitrary",)),
        name=f"merge{layer}",
    )(oa_c, op_c, oa_s, op_s, gates, gates, *x_parts, mods, w_ba, w_bp, w_out,
      norm2_g.reshape(DEPTH, 1, D_MODEL), router_wt)


def _route_kernel(lg_ref, rb_ref, tri_ref, idx_ref, w_ref, rank_ref, cnt_ref, carry_ref):
    @pl.when(pl.program_id(0) == 0)
    def _():
        carry_ref[...] = jnp.zeros_like(carry_ref)

    scores = jax.nn.sigmoid(lg_ref[...])
    sel = scores + rb_ref[...]
    gscore = []
    for g in range(N_GROUPS):
        a, b, c, d = [sel[EXPERTS_PER_GROUP * g + i:EXPERTS_PER_GROUP * g + i + 1, :] for i in range(4)]
        hi1, lo1, hi2, lo2 = jnp.maximum(a, b), jnp.minimum(a, b), jnp.maximum(c, d), jnp.minimum(c, d)
        gscore.append(jnp.maximum(hi1, hi2) + jnp.maximum(jnp.minimum(hi1, hi2), jnp.maximum(lo1, lo2)))
    best = jnp.zeros_like(gscore[0], dtype=jnp.int32)
    bestv = gscore[0]
    for g in range(1, N_GROUPS):
        upd = gscore[g] > bestv
        best = jnp.where(upd, g, best)
        bestv = jnp.where(upd, gscore[g], bestv)
    row = lax.broadcasted_iota(jnp.int32, sel.shape, 0)
    masked = jnp.where(row // EXPERTS_PER_GROUP == best, sel, -jnp.inf)
    i0 = jnp.min(jnp.where(masked == masked.max(axis=0, keepdims=True), row, N_EXPERTS), axis=0, keepdims=True)
    masked = jnp.where(row == i0, -jnp.inf, masked)
    i1 = jnp.min(jnp.where(masked == masked.max(axis=0, keepdims=True), row, N_EXPERTS), axis=0, keepdims=True)
    s0 = jnp.sum(jnp.where(row == i0, scores, 0.0), axis=0, keepdims=True)
    s1 = jnp.sum(jnp.where(row == i1, scores, 0.0), axis=0, keepdims=True)
    den = s0 + s1
    idx_ref[...] = jnp.concatenate([i0, i1], axis=0)
    w_ref[...] = jnp.concatenate([s0 / den, s1 / den], axis=0)

    hit = jnp.where((row == i0) | (row == i1), 1.0, 0.0)
    before = _dot(hit.astype(BF16), tri_ref[...]) + carry_ref[:, 0:1]
    r0 = jnp.sum(jnp.where(row == i0, before, 0.0), axis=0, keepdims=True)
    r1 = jnp.sum(jnp.where(row == i1, before, 0.0), axis=0, keepdims=True)
    rank_ref[...] = jnp.concatenate([r0, r1], axis=0).astype(jnp.int32)
    carry_ref[...] = carry_ref[...] + jnp.sum(hit, axis=1, keepdims=True)
    cnt_ref[...] = carry_ref[...]


def _route(logits_t, router_bias):
    tn = ROUTE_BLOCK
    tri = jnp.asarray(np.triu(np.ones((tn, tn), np.float32), k=1), BF16)
    pair = lambda dt: jax.ShapeDtypeStruct((TOP_K, T_ALL), dt)
    return pl.pallas_call(
        _route_kernel,
        grid=(T_ALL // tn,),
        in_specs=[pl.BlockSpec((N_EXPERTS, tn), lambda i: (0, i)),
                  pl.BlockSpec((N_EXPERTS, 1), lambda i: (0, 0)),
                  pl.BlockSpec((tn, tn), lambda i: (0, 0))],
        out_specs=[pl.BlockSpec((TOP_K, tn), lambda i: (0, i)),
                   pl.BlockSpec((TOP_K, tn), lambda i: (0, i)),
                   pl.BlockSpec((TOP_K, tn), lambda i: (0, i)),
                   pl.BlockSpec((N_EXPERTS, LANES), lambda i: (0, 0))],
        out_shape=[pair(jnp.int32), pair(F32), pair(jnp.int32),
                   jax.ShapeDtypeStruct((N_EXPERTS, LANES), F32)],
        scratch_shapes=[pltpu.VMEM((N_EXPERTS, LANES), F32)],
        compiler_params=_params(("arbitrary",)),
        name="route",
    )(logits_t, router_bias.reshape(N_EXPERTS, 1).astype(F32), tri)


def _slot_kernel(idx_ref, rank_ref, off_ref, pos_ref):
    row = lax.broadcasted_iota(jnp.int32, (N_EXPERTS, idx_ref.shape[1]), 0)
    for j in range(TOP_K):
        off = jnp.sum(jnp.where(row == idx_ref[j:j + 1, :], off_ref[...], 0), axis=0, keepdims=True)
        pos_ref[j:j + 1, :] = off + rank_ref[j:j + 1, :]


def _slots(idx_t, rank_t, seg_start):
    tn = ROUTE_BLOCK
    blk = pl.BlockSpec((TOP_K, tn), lambda i: (0, i))
    return pl.pallas_call(
        _slot_kernel,
        grid=(T_ALL // tn,),
        in_specs=[blk, blk, pl.BlockSpec((N_EXPERTS, 1), lambda i: (0, 0))],
        out_specs=blk,
        out_shape=jax.ShapeDtypeStruct((TOP_K, T_ALL), jnp.int32),
        compiler_params=_params(("arbitrary",)),
        name="slots",
    )(idx_t, rank_t, seg_start.reshape(N_EXPERTS, 1))


def _tile_plan(counts):
    cnt = counts.astype(jnp.int32)
    ntile = (cnt + MOE_TM - 1) // MOE_TM
    tile_end = jnp.cumsum(ntile)
    seg_start = (tile_end - ntile) * MOE_TM
    n_used = tile_end[-1]
    tiles = jnp.arange(MOE_TILES, dtype=jnp.int32)
    live = jnp.minimum(tiles, n_used - 1)
    expert = jnp.sum((live[:, None] >= tile_end[None, :]).astype(jnp.int32), axis=1)
    first = ((live == (tile_end - ntile)[expert]) & (tiles < n_used)).astype(jnp.int32)
    return seg_start, expert, first, n_used.reshape(1)


SUBLANES = 8


def _row_copy(src, i, dst, tile, sub, sem):
    return pltpu.make_async_copy(src.at[pl.ds(i, 1), :], dst.at[tile, pl.ds(sub, 1), :], sem)


def _gather_rows(src, index_of, dst, n_rows, sem):
    def body(tile, carry):
        for sub in range(SUBLANES):
            _row_copy(src, index_of(tile * SUBLANES + sub), dst, tile, sub, sem).start()
        return carry
    lax.fori_loop(0, n_rows // SUBLANES, body, 0)


def _wait_rows(src, dst, n_rows, sem):
    for _ in range(n_rows):
        _row_copy(src, 0, dst, 0, 0, sem).wait()


def _invert_kernel(pos0_ref, pos1_ref, src_ref):
    def fill(s, carry):
        src_ref[s] = 0
        return carry

    def put(t, carry):
        src_ref[pos0_ref[t]] = t
        src_ref[pos1_ref[t]] = t
        return carry

    lax.fori_loop(0, MOE_TILES * MOE_TM, fill, 0, unroll=8)
    lax.fori_loop(0, T_ALL, put, 0, unroll=8)


def _invert(pos):
    return pl.pallas_call(
        _invert_kernel,
        grid_spec=pltpu.PrefetchScalarGridSpec(
            num_scalar_prefetch=2,
            grid=(1,),
            in_specs=[],
            out_specs=pl.BlockSpec(memory_space=pltpu.SMEM),
        ),
        out_shape=jax.ShapeDtypeStruct((MOE_TILES * MOE_TM,), jnp.int32),
        compiler_params=_params(("arbitrary",)),
        name="invert",
    )(pos[0], pos[1])


def _moe_kernel(src_ref, expert_ref, first_ref, nused_ref, h_ref, wg_ref, wu_ref, wd_ref, y_ref,
                xbuf, sems, wg_bf, wu_bf, wd_bf):
    i = pl.program_id(0)
    n_used = nused_ref[0]
    slot = i % 2

    def issue(tile, slot):
        _gather_rows(h_ref, lambda r: src_ref[tile * MOE_TM + r], xbuf.at[slot], MOE_TM, sems.at[slot])

    @pl.when(i == 0)
    def _():
        issue(0, 0)

    @pl.when(i + 1 < n_used)
    def _():
        issue(i + 1, 1 - slot)

    @pl.when(first_ref[i] == 1)
    def _():
        wg_bf[...] = wg_ref[0, 0].astype(BF16)
        wu_bf[...] = wu_ref[0, 0].astype(BF16)
        wd_bf[...] = wd_ref[0, 0].astype(BF16)

    @pl.when(i < n_used)
    def _():
        _wait_rows(h_ref, xbuf.at[slot], MOE_TM, sems.at[slot])
        x = xbuf[slot].reshape(MOE_TM, D_MODEL).astype(BF16)
        act = jax.nn.silu(_dot(x, wg_bf[...])) * _dot(x, wu_bf[...])
        y_ref[...] = _dot(act.astype(BF16), wd_bf[...])

    @pl.when(i >= n_used)
    def _():
        y_ref[...] = jnp.zeros_like(y_ref)


def _moe(h2, src, expert, first, n_used, w_gate, w_up, w_down, layer):
    wspec = lambda shape: pl.BlockSpec((1, 1) + shape, lambda i, sr, ex, fi, nu: (layer, ex[i], 0, 0))
    return pl.pallas_call(
        _moe_kernel,
        grid_spec=pltpu.PrefetchScalarGridSpec(
            num_scalar_prefetch=4,
            grid=(MOE_TILES,),
            in_specs=[pl.BlockSpec(memory_space=pl.ANY),
                      wspec((D_MODEL, EXPERT_FF)), wspec((D_MODEL, EXPERT_FF)), wspec((EXPERT_FF, D_MODEL))],
            out_specs=pl.BlockSpec((MOE_TM, D_MODEL), lambda i, sr, ex, fi, nu: (i, 0)),
            scratch_shapes=[pltpu.VMEM((2, MOE_TM // SUBLANES, SUBLANES, D_MODEL), F32),
                            pltpu.SemaphoreType.DMA((2,)),
                            pltpu.VMEM((D_MODEL, EXPERT_FF), BF16), pltpu.VMEM((D_MODEL, EXPERT_FF), BF16),
                            pltpu.VMEM((EXPERT_FF, D_MODEL), BF16)],
        ),
        out_shape=jax.ShapeDtypeStruct((MOE_TILES * MOE_TM, D_MODEL), F32),
        compiler_params=_params(("arbitrary",), disable_bounds_checks=True),
        name=f"moe{layer}",
    )(src, expert, first, n_used, h2, w_gate, w_up, w_down)


def _combine_kernel(last, pos0_ref, pos1_ref, ys_ref, x_ref, w_ref, mod_ref, g_ref, nmod_ref, *refs):
    outs, (ybuf, sems) = refs[:2], refs[2:]
    i = pl.program_id(0)
    n = pl.num_programs(0)
    tm = x_ref.shape[0]

    def issue(tile, slot):
        for choice, pos_ref in enumerate((pos0_ref, pos1_ref)):
            _gather_rows(ys_ref, lambda r: pos_ref[tile * tm + r], ybuf.at[slot, choice], tm, sems.at[slot])

    slot = i % 2

    @pl.when(i == 0)
    def _():
        issue(0, 0)

    @pl.when(i + 1 < n)
    def _():
        issue(i + 1, 1 - slot)

    _wait_rows(ys_ref, ybuf.at[slot, 0], TOP_K * tm, sems.at[slot])

    w0 = _row_to_col(w_ref[0:1, :])
    w1 = _row_to_col(w_ref[1:2, :])
    y = w0 * ybuf[slot, 0].reshape(tm, D_MODEL) + w1 * ybuf[slot, 1].reshape(tm, D_MODEL)
    x = x_ref[...] + mod_ref[0, 0, G2:G2 + 1, :] * y
    if last:
        yc_ref, ys_out_ref = outs
        final = _rms(x) * g_ref[0]

        @pl.when(i < T_CTX // tm)
        def _():
            yc_ref[...] = final

        @pl.when(i >= T_CTX // tm)
        def _():
            ys_out_ref[...] = final
    else:
        xo_ref, h_ref = outs
        xo_ref[...] = x
        h_ref[...] = _mod_norm(x, g_ref[0], nmod_ref, SH1, SC1).astype(BF16)


def _combine(ys, pos, w_t, x, mods, norm_g, layer):
    tm = COMBINE_TM
    last = layer == DEPTH - 1
    nxt = 0 if last else layer + 1
    tok = pl.BlockSpec((tm, D_MODEL), lambda i, p0, p1: (i, 0))
    if last:
        out_specs = _two_group_specs(tm, D_MODEL)
        out_shape = [jax.ShapeDtypeStruct((T_CTX, D_MODEL), F32), jax.ShapeDtypeStruct((T_SMP, D_MODEL), F32)]
    else:
        out_specs = [tok, tok]
        out_shape = [jax.ShapeDtypeStruct((T_ALL, D_MODEL), F32), jax.ShapeDtypeStruct((T_ALL, D_MODEL), BF16)]
    return pl.pallas_call(
        functools.partial(_combine_kernel, last),
        grid_spec=pltpu.PrefetchScalarGridSpec(
            num_scalar_prefetch=2,
            grid=(T_ALL // tm,),
            in_specs=[pl.BlockSpec(memory_space=pl.ANY),
                      tok,
                      pl.BlockSpec((TOP_K, tm), lambda i, p0, p1: (0, i)),
                      _mod_spec(layer, tm),
                      pl.BlockSpec((1, 1, D_MODEL), lambda i, p0, p1: (nxt, 0, 0)),
                      _mod_spec(nxt, tm)],
            out_specs=out_specs,
            scratch_shapes=[pltpu.VMEM((2, TOP_K, tm // SUBLANES, SUBLANES, D_MODEL), F32),
                            pltpu.SemaphoreType.DMA((2,))],
        ),
        out_shape=out_shape,
        compiler_params=_params(("arbitrary",), disable_bounds_checks=True),
        name=f"combine{layer}",
    )(pos[0], pos[1], ys, x, w_t, mods, norm_g, mods)


def kernel(x_prompt, x_sample, c, cache_k, cache_v, c_ctx, ada_w, ada_b, norm1_g, w_in, rpb, pool_w, pool_scale,
           w_branch_a, w_branch_p, w_out, norm2_g, router_w, router_bias, moe_w_gate, moe_w_up, moe_w_down, final_g):
    x_ctx = x_prompt.reshape(T_CTX, D_MODEL)
    x_smp = x_sample.reshape(T_SMP, D_MODEL)
    cond = jnp.concatenate([c_ctx[None, :], c, jnp.zeros((N_COND - 1 - DEC_BATCH, D_MODEL), F32)], axis=0)
    mods = _ada(cond, ada_w, ada_b).reshape(DEPTH, N_COND, N_MOD, D_MODEL)
    rpb_pad = jnp.pad(rpb.astype(F32), ((0, 0), (0, 0), (0, 1), (0, LANES - rpb.shape[-1])))
    w_ba, w_bp, w_o = w_branch_a.astype(BF16), w_branch_p.astype(BF16), w_out.astype(BF16)
    router_wt = router_w.T
    norm1 = norm1_g.reshape(DEPTH, 1, D_MODEL)

    h = _pre(x_ctx, x_smp, norm1_g, mods)
    x_parts = (x_ctx, x_smp)
    new_ks, new_vs = [], []
    for l in range(DEPTH):
        q, kc, ks, vc, vs, u, gates = _inproj(h, w_in, l)
        new_ks.append(kc.reshape(BATCH, SEQ, NA_HEADS, NA_HEAD_DIM))
        new_vs.append(vc.reshape(BATCH, SEQ, NA_HEADS, NA_HEAD_DIM))
        oa_c, op_c = _ctx_mixer(q, kc, vc, u, pool_w, pool_scale, l)
        oa_s, op_s = _na_mixer(q, ks, vs, u, cache_k, cache_v, rpb_pad, pool_w, pool_scale, l)
        x, h2, logits_t = _merge(oa_c, op_c, oa_s, op_s, gates, x_parts, mods, w_ba, w_bp, w_o, norm2_g,
                                 router_wt, l)
        idx_t, w_t, rank_t, counts = _route(logits_t, router_bias)
        seg_start, expert, first, n_used = _tile_plan(counts[:, 0])
        pos = _slots(idx_t, rank_t, seg_start)
        ys = _moe(h2, _invert(pos), expert, first, n_used, moe_w_gate, moe_w_up, moe_w_down, l)
        if l < DEPTH - 1:
            x, h = _combine(ys, pos, w_t, x, mods, norm1, l)
            x_parts = (x,)
        else:
            y_ctx, y_smp = _combine(ys, pos, w_t, x, mods, final_g.reshape(1, 1, D_MODEL), l)
    return (y_ctx.reshape(BATCH, SEQ, D_MODEL), y_smp.reshape(DEC_BATCH, DEC_SEQ, D_MODEL),
            jnp.stack(new_ks, axis=1), jnp.stack(new_vs, axis=1))
```

```python
import functools

import numpy as np
import jax
import jax.numpy as jnp
from jax import lax
from jax.experimental import pallas as pl
from jax.experimental.pallas import tpu as pltpu

D_MODEL = 2048
BATCH = 32
SEQ = 256
DEPTH = 2
DEC_BATCH = 2
DEC_SEQ = 2048
PAST_LEN = 256
GRID_W = 64
GRID_ROWS = DEC_SEQ // GRID_W
NA_HEADS = 16
NA_HEAD_DIM = 64
NA_WIDTH = NA_HEADS * NA_HEAD_DIM
WIN_H = 8
WIN_W = 16
POOL_WIDTH = D_MODEL // 2
POOL_WINDOWS = (2, 4, 8, 16)
POOL_GROUPS = len(POOL_WINDOWS)
POOL_GC = POOL_WIDTH // POOL_GROUPS
IN_WIDTH = 3 * NA_WIDTH + POOL_WIDTH + 2 * D_MODEL
N_EXPERTS = 16
N_GROUPS = 4
EXPERTS_PER_GROUP = N_EXPERTS // N_GROUPS
TOP_K = 2
EXPERT_FF = D_MODEL // 4
N_MOD = 6
EPS = 1e-6

T_CTX = BATCH * SEQ
T_SMP = DEC_BATCH * DEC_SEQ
T_ALL = T_CTX + T_SMP
N_COND = 8
NA_KEYS = WIN_H * GRID_W
MASKED = -1e30

F32 = jnp.float32
BF16 = jnp.bfloat16
VMEM_LIMIT = 56 * 1024 * 1024

SH1, SC1, G1, SH2, SC2, G2 = range(6)
LANES = 128
PAIR = 2 * NA_HEAD_DIM
ROUTE_BLOCK = 512
MOE_TM = 256
MOE_TILES = T_ALL * TOP_K // MOE_TM + N_EXPERTS
COMBINE_TM = 256


def _params(sem, vmem=VMEM_LIMIT, **kw):
    return pltpu.CompilerParams(dimension_semantics=sem, vmem_limit_bytes=vmem, **kw)


def _cond_of_tile(i, tm):
    t0 = i * tm
    return jnp.where(t0 < T_CTX, 0, 1 + (t0 - T_CTX) // DEC_SEQ)


def _mod_spec(layer, tm):
    last = T_ALL // tm - 1
    return pl.BlockSpec((1, 1, N_MOD, D_MODEL),
                        lambda i, *_: (layer, _cond_of_tile(jnp.minimum(i, last), tm), 0, 0))


def _two_group_specs(tm, width):
    nc = T_CTX // tm
    ns = T_SMP // tm
    return [pl.BlockSpec((tm, width), lambda i, *_: (jnp.minimum(i, nc - 1), 0)),
            pl.BlockSpec((tm, width), lambda i, *_: (jnp.clip(i - nc, 0, ns - 1), 0))]


def _rms(x):
    return x * lax.rsqrt(jnp.mean(x * x, axis=-1, keepdims=True) + EPS)


def _mod_norm(x, g, mod_ref, shift_row, scale_row):
    return (_rms(x) * g * (1.0 + mod_ref[0, 0, scale_row:scale_row + 1, :])
            + mod_ref[0, 0, shift_row:shift_row + 1, :])


def _dot(a, b):
    return jnp.dot(a, b, preferred_element_type=F32)


def _dot_nt(a, b):
    return lax.dot_general(a, b, (((1,), (1,)), ((), ())), preferred_element_type=F32)


def _split2(x):
    x1 = x.astype(BF16)
    return x1, (x - x1.astype(F32)).astype(BF16)


def _row_to_col(v):
    n = v.shape[1]
    r = lax.broadcasted_iota(jnp.int32, (n, n), 0)
    c = lax.broadcasted_iota(jnp.int32, (n, n), 1)
    return jnp.sum(jnp.where(r == c, v, 0.0), axis=1, keepdims=True)


def _ada_kernel(c_ref, w_ref, b_ref, o_ref):
    s = jax.nn.silu(c_ref[...]).astype(BF16)
    o_ref[0] = _dot(s, w_ref[0].astype(BF16)) + b_ref[0]


def _ada(cond, ada_w, ada_b):
    tn = 1024
    nj = N_MOD * D_MODEL // tn
    return pl.pallas_call(
        _ada_kernel,
        grid=(DEPTH, nj),
        in_specs=[
            pl.BlockSpec((N_COND, D_MODEL), lambda l, j: (0, 0)),
            pl.BlockSpec((1, D_MODEL, tn), lambda l, j: (l, 0, j)),
            pl.BlockSpec((1, 1, tn), lambda l, j: (l, 0, j)),
        ],
        out_specs=pl.BlockSpec((1, N_COND, tn), lambda l, j: (l, 0, j)),
        out_shape=jax.ShapeDtypeStruct((DEPTH, N_COND, N_MOD * D_MODEL), F32),
        compiler_params=_params(("arbitrary", "arbitrary")),
        name="ada",
    )(cond, ada_w, ada_b.reshape(DEPTH, 1, N_MOD * D_MODEL))


def _pre_kernel(xc_ref, xs_ref, g_ref, mod_ref, h_ref):
    is_ctx = pl.program_id(0) < T_CTX // h_ref.shape[0]
    x = jnp.where(is_ctx, xc_ref[...], xs_ref[...])
    h_ref[...] = _mod_norm(x, g_ref[0], mod_ref, SH1, SC1).astype(BF16)


def _pre(x_ctx, x_smp, norm_g, mods):
    tm = 512
    return pl.pallas_call(
        _pre_kernel,
        grid=(T_ALL // tm,),
        in_specs=_two_group_specs(tm, D_MODEL) + [
            pl.BlockSpec((1, 1, D_MODEL), lambda i: (0, 0, 0)),
            _mod_spec(0, tm),
        ],
        out_specs=pl.BlockSpec((tm, D_MODEL), lambda i: (i, 0)),
        out_shape=jax.ShapeDtypeStruct((T_ALL, D_MODEL), BF16),
        compiler_params=_params(("arbitrary",)),
        name="pre",
    )(x_ctx, x_smp, norm_g.reshape(DEPTH, 1, D_MODEL), mods)


INPROJ_TM = 512
INPROJ_TN = 1024
_NI = T_ALL // INPROJ_TM
_NC = T_CTX // INPROJ_TM
_COLS = {name: (start // INPROJ_TN, width // INPROJ_TN) for name, start, width in (
    ("q", 0, NA_WIDTH), ("k", NA_WIDTH, NA_WIDTH), ("v", 2 * NA_WIDTH, NA_WIDTH),
    ("u", 3 * NA_WIDTH, POOL_WIDTH), ("g", 3 * NA_WIDTH + POOL_WIDTH, 2 * D_MODEL))}


def _visit(name, rows, last_row):
    j0, nj = _COLS[name]

    def index(j, i):
        row = jnp.where(j < j0, 0, jnp.where(j < j0 + nj, rows(i), last_row))
        return row, jnp.clip(j - j0, 0, nj - 1)
    return index


def _inproj_kernel(h_ref, w_ref, q_ref, kc_ref, ks_ref, vc_ref, vs_ref, u_ref, g_ref, wbf_ref):
    j = pl.program_id(0)
    i = pl.program_id(1)

    @pl.when(i == 0)
    def _():
        wbf_ref[...] = w_ref[0].astype(BF16)

    def project():
        return _dot(h_ref[...], wbf_ref[...])

    def during(name):
        j0, nj = _COLS[name]
        return (j >= j0) & (j < j0 + nj)

    @pl.when(during("q"))
    def _():
        q_ref[...] = (project() * NA_HEAD_DIM ** -0.5).astype(BF16)

    for name, ctx_ref, smp_ref in (("k", kc_ref, ks_ref), ("v", vc_ref, vs_ref)):
        @pl.when(during(name) & (i < _NC))
        def _():
            ctx_ref[...] = project()

        @pl.when(during(name) & (i >= _NC))
        def _():
            smp_ref[...] = project().astype(BF16)

    @pl.when(during("u"))
    def _():
        u_ref[...] = project()

    @pl.when(during("g"))
    def _():
        g_ref[...] = project()


def _inproj(h, w_in, layer):
    tm, tn = INPROJ_TM, INPROJ_TN
    blk = pl.BlockSpec
    every = lambda i: i
    ctx_rows = lambda i: jnp.minimum(i, _NC - 1)
    smp_rows = lambda i: jnp.clip(i - _NC, 0, _NI - _NC - 1)
    sds = jax.ShapeDtypeStruct
    return pl.pallas_call(
        _inproj_kernel,
        grid=(IN_WIDTH // tn, _NI),
        in_specs=[blk((tm, D_MODEL), lambda j, i: (i, 0)),
                  blk((1, D_MODEL, tn), lambda j, i: (layer, 0, j))],
        out_specs=[blk((tm, tn), _visit("q", every, _NI - 1)),
                   blk((tm, tn), _visit("k", ctx_rows, _NC - 1)),
                   blk((tm, tn), _visit("k", smp_rows, _NI - _NC - 1)),
                   blk((tm, tn), _visit("v", ctx_rows, _NC - 1)),
                   blk((tm, tn), _visit("v", smp_rows, _NI - _NC - 1)),
                   blk((tm, tn), _visit("u", every, _NI - 1)),
                   blk((tm, tn), _visit("g", every, _NI - 1))],
        out_shape=[sds((T_ALL, NA_WIDTH), BF16),
                   sds((T_CTX, NA_WIDTH), F32), sds((T_SMP, NA_WIDTH), BF16),
                   sds((T_CTX, NA_WIDTH), F32), sds((T_SMP, NA_WIDTH), BF16),
                   sds((T_ALL, POOL_WIDTH), F32),
                   sds((T_ALL, 2 * D_MODEL), F32)],
        scratch_shapes=[pltpu.VMEM((D_MODEL, tn), BF16)],
        compiler_params=_params(("arbitrary", "arbitrary")),
        name=f"inproj{layer}",
    )(h, w_in)


def _pool_consts(n, seg):
    t = np.arange(n)
    pos = t % seg
    base = t - pos
    mats = np.zeros((POOL_GROUPS, n, n), np.float32)
    cnts = np.zeros((POOL_GROUPS, n, 1), np.float32)
    for gi, w in enumerate(POOL_WINDOWS):
        lo = np.clip(pos - w // 2, 0, seg)
        hi = np.clip(pos + w - w // 2, 0, seg)
        s = t[None, :]
        mats[gi] = ((s >= (base + lo)[:, None]) & (s < (base + hi)[:, None])).astype(np.float32)
        cnts[gi, :, 0] = hi - lo
    return jnp.asarray(mats, BF16), jnp.asarray(cnts, F32)


def _pool(u_ref, band_ref, cnt_ref, pw_ref, ps_ref, op_ref):
    for gi in range(POOL_GROUPS):
        cols = slice(gi * POOL_GC, (gi + 1) * POOL_GC)
        ug = u_ref[:, cols]
        u1 = ug.astype(BF16)
        u2 = (ug - u1.astype(F32)).astype(BF16)
        u3 = (ug - u1.astype(F32) - u2.astype(F32)).astype(BF16)
        band = band_ref[gi]
        wsum = _dot(band, u1) + _dot(band, u2) + _dot(band, u3)
        d = (wsum / cnt_ref[gi] - ug).astype(BF16)
        y = _dot(d, pw_ref[0, gi].astype(BF16))
        op_ref[:, cols] = (y * ps_ref[0, :, cols]).astype(BF16)


def _softmax_av(s_parts, v_parts):
    m = s_parts[0].max(axis=-1, keepdims=True)
    for s in s_parts[1:]:
        m = jnp.maximum(m, s.max(axis=-1, keepdims=True))
    den = None
    acc = None
    for s, v in zip(s_parts, v_parts):
        p = jnp.exp(s - m)
        ps = p.sum(axis=-1, keepdims=True)
        den = ps if den is None else den + ps
        pv = _dot(p.astype(BF16), v)
        acc = pv if acc is None else acc + pv
    return acc / den


def _attend_pair(q, ks, vs, biases):
    m = q.shape[0]
    first = lax.broadcasted_iota(jnp.int32, (m, PAIR), 1) < NA_HEAD_DIM
    qf = q.astype(F32)
    q2 = jnp.concatenate([jnp.where(first, qf, 0.0), jnp.where(first, 0.0, qf)], axis=0).astype(BF16)
    scores = [_dot_nt(q2, k) if b is None else _dot_nt(q2, k) + b for k, b in zip(ks, biases)]
    o2 = _softmax_av(scores, vs)
    return jnp.where(first, o2[:m], o2[m:])


def _ctx_mixer_kernel(q_ref, k_ref, v_ref, u_ref, band_ref, cnt_ref, pw_ref, ps_ref, oa_ref, op_ref):
    for hp in range(NA_HEADS // 2):
        cols = slice(hp * PAIR, (hp + 1) * PAIR)
        o = _attend_pair(q_ref[:, cols], [k_ref[:, cols].astype(BF16)], [v_ref[:, cols].astype(BF16)], [None])
        oa_ref[:, cols] = o.astype(BF16)
    _pool(u_ref, band_ref, cnt_ref, pw_ref, ps_ref, op_ref)


def _ctx_mixer(q, kc, vc, u, pool_w, pool_scale, layer):
    band, cnt = _pool_consts(SEQ, SEQ)
    seq = lambda width: pl.BlockSpec((SEQ, width), lambda s: (s, 0))
    return pl.pallas_call(
        _ctx_mixer_kernel,
        grid=(BATCH,),
        in_specs=[
            seq(NA_WIDTH), seq(NA_WIDTH), seq(NA_WIDTH), seq(POOL_WIDTH),
            pl.BlockSpec((POOL_GROUPS, SEQ, SEQ), lambda s: (0, 0, 0)),
            pl.BlockSpec((POOL_GROUPS, SEQ, 1), lambda s: (0, 0, 0)),
            pl.BlockSpec((1, POOL_GROUPS, POOL_GC, POOL_GC), lambda s: (layer, 0, 0, 0)),
            pl.BlockSpec((1, 1, POOL_WIDTH), lambda s: (layer, 0, 0)),
        ],
        out_specs=[seq(NA_WIDTH), seq(POOL_WIDTH)],
        out_shape=[jax.ShapeDtypeStruct((T_CTX, NA_WIDTH), BF16),
                   jax.ShapeDtypeStruct((T_CTX, POOL_WIDTH), BF16)],
        compiler_params=_params(("arbitrary",)),
        name=f"ctx_mixer{layer}",
    )(q, kc, vc, u, band, cnt, pool_w, pool_scale.reshape(DEPTH, 1, POOL_WIDTH))


def _na_build_bias(rpb_ref, bias_ref, r, rs):
    lanes = 2 * GRID_W
    qc = lax.broadcasted_iota(jnp.int32, (GRID_W, lanes), 0)
    lane = lax.broadcasted_iota(jnp.int32, (GRID_W, lanes), 1)
    kc = jnp.bitwise_and(lane, GRID_W - 1)
    cstart = jnp.clip(qc - WIN_W // 2, 0, GRID_W - WIN_W)
    valid = (kc >= cstart) & (kc < cstart + WIN_W)
    for h in range(NA_HEADS):
        for jp in range(WIN_H // 2):
            halves = []
            for j, shift in ((2 * jp, lanes - (WIN_W - 1)), (2 * jp + 1, GRID_W - (WIN_W - 1))):
                dr = rs + j - r + (WIN_H - 1)
                row = jnp.broadcast_to(rpb_ref[0, h, pl.ds(dr, 1), :], (GRID_W, lanes))
                halves.append(pltpu.roll(row, shift, 1, stride=1, stride_axis=0))
            tile = jnp.where(lane < GRID_W, halves[0], halves[1])
            bias_ref[h, :, jp * lanes:(jp + 1) * lanes] = jnp.where(valid, tile, MASKED)


def _na_mixer_kernel(q_ref, k_ref, v_ref, ck_ref, cv_ref, rpb_ref, u_ref, band_ref, cnt_ref, pw_ref, ps_ref,
                     oa_ref, op_ref, bias_ref, ckb_ref, cvb_ref):
    r = pl.program_id(1)
    rs = jnp.clip(r - WIN_H // 2, 0, GRID_ROWS - WIN_H)

    @pl.when(r == 0)
    def _():
        ckb_ref[...] = ck_ref[0, 0].astype(BF16)
        cvb_ref[...] = cv_ref[0, 0].astype(BF16)

    @pl.when((r <= WIN_H // 2) | (r > GRID_ROWS - WIN_H // 2))
    def _():
        _na_build_bias(rpb_ref, bias_ref, r, rs)

    keys = pl.ds(pl.multiple_of(rs * GRID_W, GRID_W), NA_KEYS)
    for hp in range(NA_HEADS // 2):
        cols = slice(hp * PAIR, (hp + 1) * PAIR)
        bias = bias_ref[2 * hp:2 * hp + 2].reshape(2 * GRID_W, NA_KEYS)
        o = _attend_pair(q_ref[:, cols], [k_ref[keys, cols], ckb_ref[:, cols]],
                         [v_ref[keys, cols], cvb_ref[:, cols]], [bias, None])
        oa_ref[:, cols] = o.astype(BF16)
    _pool(u_ref, band_ref, cnt_ref, pw_ref, ps_ref, op_ref)


def _na_mixer(q, ks, vs, u, cache_k, cache_v, rpb_pad, pool_w, pool_scale, layer):
    band, cnt = _pool_consts(GRID_W, GRID_W)
    row0 = T_CTX // GRID_W
    row = lambda width: pl.BlockSpec((GRID_W, width), lambda b, r: (row0 + b * GRID_ROWS + r, 0))
    batch = pl.BlockSpec((DEC_SEQ, NA_WIDTH), lambda b, r: (b, 0))
    cblk = pl.BlockSpec((1, 1, PAST_LEN, NA_WIDTH), lambda b, r: (b, layer, 0, 0))
    out = lambda width: pl.BlockSpec((GRID_W, width), lambda b, r: (b * GRID_ROWS + r, 0))
    return pl.pallas_call(
        _na_mixer_kernel,
        grid=(DEC_BATCH, GRID_ROWS),
        in_specs=[
            row(NA_WIDTH), batch, batch, cblk, cblk,
            pl.BlockSpec((1,) + rpb_pad.shape[1:], lambda b, r: (layer, 0, 0, 0)),
            row(POOL_WIDTH),
            pl.BlockSpec((POOL_GROUPS, GRID_W, GRID_W), lambda b, r: (0, 0, 0)),
            pl.BlockSpec((POOL_GROUPS, GRID_W, 1), lambda b, r: (0, 0, 0)),
            pl.BlockSpec((1, POOL_GROUPS, POOL_GC, POOL_GC), lambda b, r: (layer, 0, 0, 0)),
            pl.BlockSpec((1, 1, POOL_WIDTH), lambda b, r: (layer, 0, 0)),
        ],
        out_specs=[out(NA_WIDTH), out(POOL_WIDTH)],
        out_shape=[jax.ShapeDtypeStruct((T_SMP, NA_WIDTH), BF16),
                   jax.ShapeDtypeStruct((T_SMP, POOL_WIDTH), BF16)],
        scratch_shapes=[pltpu.VMEM((NA_HEADS, GRID_W, NA_KEYS), F32),
                        pltpu.VMEM((PAST_LEN, NA_WIDTH), BF16), pltpu.VMEM((PAST_LEN, NA_WIDTH), BF16)],
        compiler_params=_params(("arbitrary", "arbitrary")),
        name=f"na_mixer{layer}",
    )(q, ks, vs,
      cache_k.reshape(DEC_BATCH, DEPTH, PAST_LEN, NA_WIDTH), cache_v.reshape(DEC_BATCH, DEPTH, PAST_LEN, NA_WIDTH),
      rpb_pad, u, band, cnt, pool_w, pool_scale.reshape(DEPTH, 1, POOL_WIDTH))


def _merge_kernel(n_x, oac_ref, opc_ref, oas_ref, ops_ref, ga_ref, gp_ref, *refs):
    x_refs, (mod_ref, wba_ref, wbp_ref, wout_ref, n2g_ref, rwt_ref, xo_ref, h2_ref, lg_ref) = refs[:n_x], refs[n_x:]
    is_ctx = pl.program_id(0) < T_CTX // xo_ref.shape[0]
    a = _dot(jnp.where(is_ctx, oac_ref[...], oas_ref[...]), wba_ref[0])
    p = _dot(jnp.where(is_ctx, opc_ref[...], ops_ref[...]), wbp_ref[0])
    mix = jax.nn.sigmoid(ga_ref[...]) * a + jax.nn.sigmoid(gp_ref[...]) * p
    m = _dot(mix.astype(BF16), wout_ref[0])
    x_in = x_refs[0][...] if n_x == 1 else jnp.where(is_ctx, x_refs[0][...], x_refs[1][...])
    x = x_in + mod_ref[0, 0, G1:G1 + 1, :] * m
    xo_ref[...] = x
    h2 = _mod_norm(x, n2g_ref[0], mod_ref, SH2, SC2)
    h2_ref[...] = h2
    w1, w2 = _split2(rwt_ref[...])
    t1, t2 = _split2(h2)
    lg_ref[...] = _dot_nt(w1, t1) + _dot_nt(w1, t2) + _dot_nt(w2, t1)


def _merge(oa_c, op_c, oa_s, op_s, gates, x_parts, mods, w_ba, w_bp, w_out, norm2_g, router_wt, layer):
    tm = 256
    full = lambda shape: pl.BlockSpec(shape, lambda i: (layer,) + (0,) * (len(shape) - 1))
    x_specs = ([pl.BlockSpec((tm, D_MODEL), lambda i: (i, 0))] if len(x_parts) == 1
               else _two_group_specs(tm, D_MODEL))
    return pl.pallas_call(
        functools.partial(_merge_kernel, len(x_parts)),
        grid=(T_ALL // tm,),
        in_specs=_two_group_specs(tm, NA_WIDTH)[:1] + _two_group_specs(tm, POOL_WIDTH)[:1]
        + _two_group_specs(tm, NA_WIDTH)[1:] + _two_group_specs(tm, POOL_WIDTH)[1:] + [
            pl.BlockSpec((tm, D_MODEL), lambda i: (i, 0)),
            pl.BlockSpec((tm, D_MODEL), lambda i: (i, 1)),
        ] + x_specs + [
            _mod_spec(layer, tm),
            full((1, NA_WIDTH, D_MODEL)),
            full((1, POOL_WIDTH, D_MODEL)),
            full((1, D_MODEL, D_MODEL)),
            full((1, 1, D_MODEL)),
            pl.BlockSpec((N_EXPERTS, D_MODEL), lambda i: (0, 0)),
        ],
        out_specs=[pl.BlockSpec((tm, D_MODEL), lambda i: (i, 0)),
                   pl.BlockSpec((tm, D_MODEL), lambda i: (i, 0)),
                   pl.BlockSpec((N_EXPERTS, tm), lambda i: (0, i))],
        out_shape=[jax.ShapeDtypeStruct((T_ALL, D_MODEL), F32),
                   jax.ShapeDtypeStruct((T_ALL, D_MODEL), F32),
                   jax.ShapeDtypeStruct((N_EXPERTS, T_ALL), F32)],
        compiler_params=_params(("arbitrary",)),
        name=f"merge{layer}",
    )(oa_c, op_c, oa_s, op_s, gates, gates, *x_parts, mods, w_ba, w_bp, w_out,
      norm2_g.reshape(DEPTH, 1, D_MODEL), router_wt)


def _route_kernel(lg_ref, rb_ref, tri_ref, idx_ref, w_ref, rank_ref, cnt_ref, carry_ref):
    @pl.when(pl.program_id(0) == 0)
    def _():
        carry_ref[...] = jnp.zeros_like(carry_ref)

    scores = jax.nn.sigmoid(lg_ref[...])
    sel = scores + rb_ref[...]
    gscore = []
    for g in range(N_GROUPS):
        a, b, c, d = [sel[EXPERTS_PER_GROUP * g + i:EXPERTS_PER_GROUP * g + i + 1, :] for i in range(4)]
        hi1, lo1, hi2, lo2 = jnp.maximum(a, b), jnp.minimum(a, b), jnp.maximum(c, d), jnp.minimum(c, d)
        gscore.append(jnp.maximum(hi1, hi2) + jnp.maximum(jnp.minimum(hi1, hi2), jnp.maximum(lo1, lo2)))
    best = jnp.zeros_like(gscore[0], dtype=jnp.int32)
    bestv = gscore[0]
    for g in range(1, N_GROUPS):
        upd = gscore[g] > bestv
        best = jnp.where(upd, g, best)
        bestv = jnp.where(upd, gscore[g], bestv)
    row = lax.broadcasted_iota(jnp.int32, sel.shape, 0)
    masked = jnp.where(row // EXPERTS_PER_GROUP == best, sel, -jnp.inf)
    i0 = jnp.min(jnp.where(masked == masked.max(axis=0, keepdims=True), row, N_EXPERTS), axis=0, keepdims=True)
    masked = jnp.where(row == i0, -jnp.inf, masked)
    i1 = jnp.min(jnp.where(masked == masked.max(axis=0, keepdims=True), row, N_EXPERTS), axis=0, keepdims=True)
    s0 = jnp.sum(jnp.where(row == i0, scores, 0.0), axis=0, keepdims=True)
    s1 = jnp.sum(jnp.where(row == i1, scores, 0.0), axis=0, keepdims=True)
    den = s0 + s1
    idx_ref[...] = jnp.concatenate([i0, i1], axis=0)
    w_ref[...] = jnp.concatenate([s0 / den, s1 / den], axis=0)

    hit = jnp.where((row == i0) | (row == i1), 1.0, 0.0)
    before = _dot(hit.astype(BF16), tri_ref[...]) + carry_ref[:, 0:1]
    r0 = jnp.sum(jnp.where(row == i0, before, 0.0), axis=0, keepdims=True)
    r1 = jnp.sum(jnp.where(row == i1, before, 0.0), axis=0, keepdims=True)
    rank_ref[...] = jnp.concatenate([r0, r1], axis=0).astype(jnp.int32)
    carry_ref[...] = carry_ref[...] + jnp.sum(hit, axis=1, keepdims=True)
    cnt_ref[...] = carry_ref[...]


def _route(logits_t, router_bias):
    tn = ROUTE_BLOCK
    tri = jnp.asarray(np.triu(np.ones((tn, tn), np.float32), k=1), BF16)
    pair = lambda dt: jax.ShapeDtypeStruct((TOP_K, T_ALL), dt)
    return pl.pallas_call(
        _route_kernel,
        grid=(T_ALL // tn,),
        in_specs=[pl.BlockSpec((N_EXPERTS, tn), lambda i: (0, i)),
                  pl.BlockSpec((N_EXPERTS, 1), lambda i: (0, 0)),
                  pl.BlockSpec((tn, tn), lambda i: (0, 0))],
        out_specs=[pl.BlockSpec((TOP_K, tn), lambda i: (0, i)),
                   pl.BlockSpec((TOP_K, tn), lambda i: (0, i)),
                   pl.BlockSpec((TOP_K, tn), lambda i: (0, i)),
                   pl.BlockSpec((N_EXPERTS, LANES), lambda i: (0, 0))],
        out_shape=[pair(jnp.int32), pair(F32), pair(jnp.int32),
                   jax.ShapeDtypeStruct((N_EXPERTS, LANES), F32)],
        scratch_shapes=[pltpu.VMEM((N_EXPERTS, LANES), F32)],
        compiler_params=_params(("arbitrary",)),
        name="route",
    )(logits_t, router_bias.reshape(N_EXPERTS, 1).astype(F32), tri)


def _slot_kernel(idx_ref, rank_ref, off_ref, pos_ref):
    row = lax.broadcasted_iota(jnp.int32, (N_EXPERTS, idx_ref.shape[1]), 0)
    for j in range(TOP_K):
        off = jnp.sum(jnp.where(row == idx_ref[j:j + 1, :], off_ref[...], 0), axis=0, keepdims=True)
        pos_ref[j:j + 1, :] = off + rank_ref[j:j + 1, :]


def _slots(idx_t, rank_t, seg_start):
    tn = ROUTE_BLOCK
    blk = pl.BlockSpec((TOP_K, tn), lambda i: (0, i))
    return pl.pallas_call(
        _slot_kernel,
        grid=(T_ALL // tn,),
        in_specs=[blk, blk, pl.BlockSpec((N_EXPERTS, 1), lambda i: (0, 0))],
        out_specs=blk,
        out_shape=jax.ShapeDtypeStruct((TOP_K, T_ALL), jnp.int32),
        compiler_params=_params(("arbitrary",)),
        name="slots",
    )(idx_t, rank_t, seg_start.reshape(N_EXPERTS, 1))


def _tile_plan(counts):
    cnt = counts.astype(jnp.int32)
    ntile = (cnt + MOE_TM - 1) // MOE_TM
    tile_end = jnp.cumsum(ntile)
    seg_start = (tile_end - ntile) * MOE_TM
    n_used = tile_end[-1]
    tiles = jnp.arange(MOE_TILES, dtype=jnp.int32)
    live = jnp.minimum(tiles, n_used - 1)
    expert = jnp.sum((live[:, None] >= tile_end[None, :]).astype(jnp.int32), axis=1)
    first = ((live == (tile_end - ntile)[expert]) & (tiles < n_used)).astype(jnp.int32)
    pad_lo = jnp.concatenate([seg_start + cnt, (n_used * MOE_TM).reshape(1)])
    pad_hi = jnp.concatenate([tile_end * MOE_TM, jnp.full((1,), MOE_TILES * MOE_TM, jnp.int32)])
    return seg_start, expert, first, n_used.reshape(1), pad_lo, pad_hi


SUBLANES = 8


def _row_copy(src, i, dst, tile, sub, sem):
    return pltpu.make_async_copy(src.at[pl.ds(i, 1), :], dst.at[tile, pl.ds(sub, 1), :], sem)


def _gather_rows(src, index_of, dst, n_rows, sem):
    def body(tile, carry):
        for sub in range(SUBLANES):
            _row_copy(src, index_of(tile * SUBLANES + sub), dst, tile, sub, sem).start()
        return carry
    lax.fori_loop(0, n_rows // SUBLANES, body, 0)


def _wait_rows(src, dst, n_rows, sem):
    for _ in range(n_rows):
        _row_copy(src, 0, dst, 0, 0, sem).wait()


def _invert_kernel(pos0_ref, pos1_ref, pad_lo_ref, pad_hi_ref, src_ref):
    def fill(s, carry):
        src_ref[s] = 0
        return carry

    def put(t, carry):
        src_ref[pos0_ref[t]] = t
        src_ref[pos1_ref[t]] = t
        return carry

    for k in range(N_EXPERTS + 1):
        lax.fori_loop(pad_lo_ref[k], pad_hi_ref[k], fill, 0)
    lax.fori_loop(0, T_ALL, put, 0, unroll=16)


def _invert(pos, pad_lo, pad_hi):
    return pl.pallas_call(
        _invert_kernel,
        grid_spec=pltpu.PrefetchScalarGridSpec(
            num_scalar_prefetch=4,
            grid=(1,),
            in_specs=[],
            out_specs=pl.BlockSpec(memory_space=pltpu.SMEM),
        ),
        out_shape=jax.ShapeDtypeStruct((MOE_TILES * MOE_TM,), jnp.int32),
        compiler_params=_params(("arbitrary",)),
        name="invert",
    )(pos[0], pos[1], pad_lo, pad_hi)


def _moe_kernel(src_ref, expert_ref, first_ref, nused_ref, h_ref, wg_ref, wu_ref, wd_ref, y_ref,
                xa, xb, sems, wg_bf, wu_bf, wd_bf):
    i = pl.program_id(0)
    n_used = nused_ref[0]

    def gather(tile, buf, sem, unrolled):
        base = jnp.minimum(tile, MOE_TILES - 1) * MOE_TM
        if unrolled:
            for r in range(MOE_TM):
                _row_copy(h_ref, src_ref[base + r], buf, r // SUBLANES, r % SUBLANES, sem).start()
        else:
            _gather_rows(h_ref, lambda r: src_ref[base + r], buf, MOE_TM, sem)

    @pl.when(i == 0)
    def _():
        gather(0, xa, sems.at[0], False)
        gather(1, xb, sems.at[1], False)

    @pl.when((first_ref[jnp.minimum(i, MOE_TILES - 1)] == 1) & (i < MOE_TILES))
    def _():
        wg_bf[...] = wg_ref[0, 0].astype(BF16)
        wu_bf[...] = wu_ref[0, 0].astype(BF16)
        wd_bf[...] = wd_ref[0, 0].astype(BF16)

    for parity, buf in enumerate((xa, xb)):
        sem = sems.at[parity]
        mine = i % 2 == parity

        @pl.when((i < n_used) & mine)
        def _():
            _wait_rows(h_ref, buf, MOE_TM, sem)
            x = buf[...].reshape(MOE_TM, D_MODEL).astype(BF16)
            gather(i + 2, buf, sem, True)
            act = jax.nn.silu(_dot(x, wg_bf[...])) * _dot(x, wu_bf[...])
            y_ref[...] = _dot(act.astype(BF16), wd_bf[...])

        @pl.when((i >= n_used) & (i < n_used + 2) & mine)
        def _():
            _wait_rows(h_ref, buf, MOE_TM, sem)

    @pl.when((i >= n_used) & (i < MOE_TILES))
    def _():
        y_ref[...] = jnp.zeros_like(y_ref)


def _moe(h2, src, expert, first, n_used, w_gate, w_up, w_down, layer):
    tile = lambda i: jnp.minimum(i, MOE_TILES - 1)
    wspec = lambda shape: pl.BlockSpec((1, 1) + shape, lambda i, sr, ex, fi, nu: (layer, ex[tile(i)], 0, 0))
    xbuf = pltpu.VMEM((MOE_TM // SUBLANES, SUBLANES, D_MODEL), F32)
    return pl.pallas_call(
        _moe_kernel,
        grid_spec=pltpu.PrefetchScalarGridSpec(
            num_scalar_prefetch=4,
            grid=(MOE_TILES + 2,),
            in_specs=[pl.BlockSpec(memory_space=pl.ANY),
                      wspec((D_MODEL, EXPERT_FF)), wspec((D_MODEL, EXPERT_FF)), wspec((EXPERT_FF, D_MODEL))],
            out_specs=pl.BlockSpec((MOE_TM, D_MODEL), lambda i, sr, ex, fi, nu: (tile(i), 0)),
            scratch_shapes=[xbuf, xbuf, pltpu.SemaphoreType.DMA((2,)),
                            pltpu.VMEM((D_MODEL, EXPERT_FF), BF16), pltpu.VMEM((D_MODEL, EXPERT_FF), BF16),
                            pltpu.VMEM((EXPERT_FF, D_MODEL), BF16)],
        ),
        out_shape=jax.ShapeDtypeStruct((MOE_TILES * MOE_TM, D_MODEL), F32),
        compiler_params=_params(("arbitrary",), disable_bounds_checks=True),
        name=f"moe{layer}",
    )(src, expert, first, n_used, h2, w_gate, w_up, w_down)


def _combine_kernel(last, pos0_ref, pos1_ref, ys_ref, x_ref, w_ref, mod_ref, g_ref, nmod_ref, *refs):
    outs, (ya, yb, sems) = refs[:2], refs[2:]
    i = pl.program_id(0)
    tm = x_ref.shape[0]
    n = T_ALL // tm

    def gather(tile, buf, sem, unrolled):
        base = jnp.minimum(tile, n - 1) * tm
        for choice, pos_ref in enumerate((pos0_ref, pos1_ref)):
            if unrolled:
                for r in range(tm):
                    _row_copy(ys_ref, pos_ref[base + r], buf.at[choice], r // SUBLANES, r % SUBLANES, sem).start()
            else:
                _gather_rows(ys_ref, lambda r: pos_ref[base + r], buf.at[choice], tm, sem)

    @pl.when(i == 0)
    def _():
        gather(0, ya, sems.at[0], False)
        gather(1, yb, sems.at[1], False)

    for parity, buf in enumerate((ya, yb)):
        sem = sems.at[parity]
        mine = i % 2 == parity

        @pl.when((i < n) & mine)
        def _():
            _wait_rows(ys_ref, buf.at[0], TOP_K * tm, sem)
            w0 = _row_to_col(w_ref[0:1, :])
            w1 = _row_to_col(w_ref[1:2, :])
            y = w0 * buf[0].reshape(tm, D_MODEL) + w1 * buf[1].reshape(tm, D_MODEL)
            gather(i + 2, buf, sem, True)
            x = x_ref[...] + mod_ref[0, 0, G2:G2 + 1, :] * y
            if last:
                yc_ref, ys_out_ref = outs
                final = _rms(x) * g_ref[0]

                @pl.when(i < T_CTX // tm)
                def _():
                    yc_ref[...] = final

                @pl.when(i >= T_CTX // tm)
                def _():
                    ys_out_ref[...] = final
            else:
                xo_ref, h_ref = outs
                xo_ref[...] = x
                h_ref[...] = _mod_norm(x, g_ref[0], nmod_ref, SH1, SC1).astype(BF16)

        @pl.when((i >= n) & mine)
        def _():
            _wait_rows(ys_ref, buf.at[0], TOP_K * tm, sem)


def _combine(ys, pos, w_t, x, mods, norm_g, layer):
    tm = COMBINE_TM
    n = T_ALL // tm
    last = layer == DEPTH - 1
    nxt = 0 if last else layer + 1
    tile = lambda i: jnp.minimum(i, n - 1)
    tok = pl.BlockSpec((tm, D_MODEL), lambda i, p0, p1: (tile(i), 0))
    if last:
        out_specs = _two_group_specs(tm, D_MODEL)
        out_shape = [jax.ShapeDtypeStruct((T_CTX, D_MODEL), F32), jax.ShapeDtypeStruct((T_SMP, D_MODEL), F32)]
    else:
        out_specs = [tok, tok]
        out_shape = [jax.ShapeDtypeStruct((T_ALL, D_MODEL), F32), jax.ShapeDtypeStruct((T_ALL, D_MODEL), BF16)]
    ybuf = pltpu.VMEM((TOP_K, tm // SUBLANES, SUBLANES, D_MODEL), F32)
    return pl.pallas_call(
        functools.partial(_combine_kernel, last),
        grid_spec=pltpu.PrefetchScalarGridSpec(
            num_scalar_prefetch=2,
            grid=(n + 2,),
            in_specs=[pl.BlockSpec(memory_space=pl.ANY),
                      tok,
                      pl.BlockSpec((TOP_K, tm), lambda i, p0, p1: (0, tile(i))),
                      _mod_spec(layer, tm),
                      pl.BlockSpec((1, 1, D_MODEL), lambda i, p0, p1: (nxt, 0, 0)),
                      _mod_spec(nxt, tm)],
            out_specs=out_specs,
            scratch_shapes=[ybuf, ybuf, pltpu.SemaphoreType.DMA((2,))],
        ),
        out_shape=out_shape,
        compiler_params=_params(("arbitrary",), disable_bounds_checks=True),
        name=f"combine{layer}",
    )(pos[0], pos[1], ys, x, w_t, mods, norm_g, mods)


def kernel(x_prompt, x_sample, c, cache_k, cache_v, c_ctx, ada_w, ada_b, norm1_g, w_in, rpb, pool_w, pool_scale,
           w_branch_a, w_branch_p, w_out, norm2_g, router_w, router_bias, moe_w_gate, moe_w_up, moe_w_down, final_g):
    x_ctx = x_prompt.reshape(T_CTX, D_MODEL)
    x_smp = x_sample.reshape(T_SMP, D_MODEL)
    cond = jnp.concatenate([c_ctx[None, :], c, jnp.zeros((N_COND - 1 - DEC_BATCH, D_MODEL), F32)], axis=0)
    mods = _ada(cond, ada_w, ada_b).reshape(DEPTH, N_COND, N_MOD, D_MODEL)
    rpb_pad = jnp.pad(rpb.astype(F32), ((0, 0), (0, 0), (0, 1), (0, LANES - rpb.shape[-1])))
    w_ba, w_bp, w_o = w_branch_a.astype(BF16), w_branch_p.astype(BF16), w_out.astype(BF16)
    router_wt = router_w.T
    norm1 = norm1_g.reshape(DEPTH, 1, D_MODEL)

    h = _pre(x_ctx, x_smp, norm1_g, mods)
    x_parts = (x_ctx, x_smp)
    new_ks, new_vs = [], []
    for l in range(DEPTH):
        q, kc, ks, vc, vs, u, gates = _inproj(h, w_in, l)
        new_ks.append(kc.reshape(BATCH, SEQ, NA_HEADS, NA_HEAD_DIM))
        new_vs.append(vc.reshape(BATCH, SEQ, NA_HEADS, NA_HEAD_DIM))
        oa_c, op_c = _ctx_mixer(q, kc, vc, u, pool_w, pool_scale, l)
        oa_s, op_s = _na_mixer(q, ks, vs, u, cache_k, cache_v, rpb_pad, pool_w, pool_scale, l)
        x, h2, logits_t = _merge(oa_c, op_c, oa_s, op_s, gates, x_parts, mods, w_ba, w_bp, w_o, norm2_g,
                                 router_wt, l)
        idx_t, w_t, rank_t, counts = _route(logits_t, router_bias)
        seg_start, expert, first, n_used, pad_lo, pad_hi = _tile_plan(counts[:, 0])
        pos = _slots(idx_t, rank_t, seg_start)
        src = _invert(pos, pad_lo, pad_hi)
        ys = _moe(h2, src, expert, first, n_used, moe_w_gate, moe_w_up, moe_w_down, l)
        if l < DEPTH - 1:
            x, h = _combine(ys, pos, w_t, x, mods, norm1, l)
            x_parts = (x,)
        else:
            y_ctx, y_smp = _combine(ys, pos, w_t, x, mods, final_g.reshape(1, 1, D_MODEL), l)
    return (y_ctx.reshape(BATCH, SEQ, D_MODEL), y_smp.reshape(DEC_BATCH, DEC_SEQ, D_MODEL),
            jnp.stack(new_ks, axis=1), jnp.stack(new_vs, axis=1))
```

```python
import functools

import numpy as np
import jax
import jax.numpy as jnp
from jax import lax
from jax.experimental import pallas as pl
from jax.experimental.pallas import tpu as pltpu

D_MODEL = 2048
BATCH = 32
SEQ = 256
DEPTH = 2
DEC_BATCH = 2
DEC_SEQ = 2048
PAST_LEN = 256
GRID_W = 64
GRID_ROWS = DEC_SEQ // GRID_W
NA_HEADS = 16
NA_HEAD_DIM = 64
NA_WIDTH = NA_HEADS * NA_HEAD_DIM
WIN_H = 8
WIN_W = 16
POOL_WIDTH = D_MODEL // 2
POOL_WINDOWS = (2, 4, 8, 16)
POOL_GROUPS = len(POOL_WINDOWS)
POOL_GC = POOL_WIDTH // POOL_GROUPS
IN_WIDTH = 3 * NA_WIDTH + POOL_WIDTH + 2 * D_MODEL
N_EXPERTS = 16
N_GROUPS = 4
EXPERTS_PER_GROUP = N_EXPERTS // N_GROUPS
TOP_K = 2
EXPERT_FF = D_MODEL // 4
N_MOD = 6
EPS = 1e-6

T_CTX = BATCH * SEQ
T_SMP = DEC_BATCH * DEC_SEQ
T_ALL = T_CTX + T_SMP
N_COND = 8
NA_KEYS = WIN_H * GRID_W
MASKED = -1e30

F32 = jnp.float32
BF16 = jnp.bfloat16
VMEM_LIMIT = 56 * 1024 * 1024

SH1, SC1, G1, SH2, SC2, G2 = range(6)
LANES = 128
PAIR = 2 * NA_HEAD_DIM
ROUTE_BLOCK = 512
MOE_TM = 256
MOE_TILES = T_ALL * TOP_K // MOE_TM + N_EXPERTS
COMBINE_TM = 256


def _params(sem, vmem=VMEM_LIMIT, **kw):
    return pltpu.CompilerParams(dimension_semantics=sem, vmem_limit_bytes=vmem, **kw)


def _cond_of_tile(i, tm):
    t0 = i * tm
    return jnp.where(t0 < T_CTX, 0, 1 + (t0 - T_CTX) // DEC_SEQ)


def _mod_spec(layer, tm):
    last = T_ALL // tm - 1
    return pl.BlockSpec((1, 1, N_MOD, D_MODEL),
                        lambda i, *_: (layer, _cond_of_tile(jnp.minimum(i, last), tm), 0, 0))


def _two_group_specs(tm, width):
    nc = T_CTX // tm
    ns = T_SMP // tm
    return [pl.BlockSpec((tm, width), lambda i, *_: (jnp.minimum(i, nc - 1), 0)),
            pl.BlockSpec((tm, width), lambda i, *_: (jnp.clip(i - nc, 0, ns - 1), 0))]


def _rms(x):
    return x * lax.rsqrt(jnp.mean(x * x, axis=-1, keepdims=True) + EPS)


def _mod_norm(x, g, mod_ref, shift_row, scale_row):
    return (_rms(x) * g * (1.0 + mod_ref[0, 0, scale_row:scale_row + 1, :])
            + mod_ref[0, 0, shift_row:shift_row + 1, :])


def _dot(a, b):
    return jnp.dot(a, b, preferred_element_type=F32)


def _dot_nt(a, b):
    return lax.dot_general(a, b, (((1,), (1,)), ((), ())), preferred_element_type=F32)


def _split2(x):
    x1 = x.astype(BF16)
    return x1, (x - x1.astype(F32)).astype(BF16)


def _row_to_col(v):
    n = v.shape[1]
    r = lax.broadcasted_iota(jnp.int32, (n, n), 0)
    c = lax.broadcasted_iota(jnp.int32, (n, n), 1)
    return jnp.sum(jnp.where(r == c, v, 0.0), axis=1, keepdims=True)


def _ada_kernel(c_ref, w_ref, b_ref, o_ref):
    s = jax.nn.silu(c_ref[...]).astype(BF16)
    o_ref[0] = _dot(s, w_ref[0].astype(BF16)) + b_ref[0]


def _ada(cond, ada_w, ada_b):
    tn = 1024
    nj = N_MOD * D_MODEL // tn
    return pl.pallas_call(
        _ada_kernel,
        grid=(DEPTH, nj),
        in_specs=[
            pl.BlockSpec((N_COND, D_MODEL), lambda l, j: (0, 0)),
            pl.BlockSpec((1, D_MODEL, tn), lambda l, j: (l, 0, j)),
            pl.BlockSpec((1, 1, tn), lambda l, j: (l, 0, j)),
        ],
        out_specs=pl.BlockSpec((1, N_COND, tn), lambda l, j: (l, 0, j)),
        out_shape=jax.ShapeDtypeStruct((DEPTH, N_COND, N_MOD * D_MODEL), F32),
        compiler_params=_params(("arbitrary", "arbitrary")),
        name="ada",
    )(cond, ada_w, ada_b.reshape(DEPTH, 1, N_MOD * D_MODEL))


def _pre_kernel(xc_ref, xs_ref, g_ref, mod_ref, h_ref):
    is_ctx = pl.program_id(0) < T_CTX // h_ref.shape[0]
    x = jnp.where(is_ctx, xc_ref[...], xs_ref[...])
    h_ref[...] = _mod_norm(x, g_ref[0], mod_ref, SH1, SC1).astype(BF16)


def _pre(x_ctx, x_smp, norm_g, mods):
    tm = 512
    return pl.pallas_call(
        _pre_kernel,
        grid=(T_ALL // tm,),
        in_specs=_two_group_specs(tm, D_MODEL) + [
            pl.BlockSpec((1, 1, D_MODEL), lambda i: (0, 0, 0)),
            _mod_spec(0, tm),
        ],
        out_specs=pl.BlockSpec((tm, D_MODEL), lambda i: (i, 0)),
        out_shape=jax.ShapeDtypeStruct((T_ALL, D_MODEL), BF16),
        compiler_params=_params(("arbitrary",)),
        name="pre",
    )(x_ctx, x_smp, norm_g.reshape(DEPTH, 1, D_MODEL), mods)


INPROJ_TM = 512
INPROJ_TN = 1024
_NI = T_ALL // INPROJ_TM
_NC = T_CTX // INPROJ_TM
_COLS = {name: (start // INPROJ_TN, width // INPROJ_TN) for name, start, width in (
    ("q", 0, NA_WIDTH), ("k", NA_WIDTH, NA_WIDTH), ("v", 2 * NA_WIDTH, NA_WIDTH),
    ("u", 3 * NA_WIDTH, POOL_WIDTH), ("g", 3 * NA_WIDTH + POOL_WIDTH, 2 * D_MODEL))}


def _visit(name, rows, last_row):
    j0, nj = _COLS[name]

    def index(j, i):
        row = jnp.where(j < j0, 0, jnp.where(j < j0 + nj, rows(i), last_row))
        return row, jnp.clip(j - j0, 0, nj - 1)
    return index


def _inproj_kernel(h_ref, w_ref, q_ref, kc_ref, ks_ref, vc_ref, vs_ref, u_ref, g_ref, wbf_ref):
    j = pl.program_id(0)
    i = pl.program_id(1)

    @pl.when(i == 0)
    def _():
        wbf_ref[...] = w_ref[0].astype(BF16)

    def project():
        return _dot(h_ref[...], wbf_ref[...])

    def during(name):
        j0, nj = _COLS[name]
        return (j >= j0) & (j < j0 + nj)

    @pl.when(during("q"))
    def _():
        q_ref[...] = (project() * NA_HEAD_DIM ** -0.5).astype(BF16)

    for name, ctx_ref, smp_ref in (("k", kc_ref, ks_ref), ("v", vc_ref, vs_ref)):
        @pl.when(during(name) & (i < _NC))
        def _():
            ctx_ref[...] = project()

        @pl.when(during(name) & (i >= _NC))
        def _():
            smp_ref[...] = project().astype(BF16)

    @pl.when(during("u"))
    def _():
        u_ref[...] = project()

    @pl.when(during("g"))
    def _():
        g_ref[...] = project().astype(BF16)


def _inproj(h, w_in, layer):
    tm, tn = INPROJ_TM, INPROJ_TN
    blk = pl.BlockSpec
    every = lambda i: i
    ctx_rows = lambda i: jnp.minimum(i, _NC - 1)
    smp_rows = lambda i: jnp.clip(i - _NC, 0, _NI - _NC - 1)
    sds = jax.ShapeDtypeStruct
    return pl.pallas_call(
        _inproj_kernel,
        grid=(IN_WIDTH // tn, _NI),
        in_specs=[blk((tm, D_MODEL), lambda j, i: (i, 0)),
                  blk((1, D_MODEL, tn), lambda j, i: (layer, 0, j))],
        out_specs=[blk((tm, tn), _visit("q", every, _NI - 1)),
                   blk((tm, tn), _visit("k", ctx_rows, _NC - 1)),
                   blk((tm, tn), _visit("k", smp_rows, _NI - _NC - 1)),
                   blk((tm, tn), _visit("v", ctx_rows, _NC - 1)),
                   blk((tm, tn), _visit("v", smp_rows, _NI - _NC - 1)),
                   blk((tm, tn), _visit("u", every, _NI - 1)),
                   blk((tm, tn), _visit("g", every, _NI - 1))],
        out_shape=[sds((T_ALL, NA_WIDTH), BF16),
                   sds((T_CTX, NA_WIDTH), F32), sds((T_SMP, NA_WIDTH), BF16),
                   sds((T_CTX, NA_WIDTH), F32), sds((T_SMP, NA_WIDTH), BF16),
                   sds((T_ALL, POOL_WIDTH), F32),
                   sds((T_ALL, 2 * D_MODEL), BF16)],
        scratch_shapes=[pltpu.VMEM((D_MODEL, tn), BF16)],
        compiler_params=_params(("arbitrary", "arbitrary")),
        name=f"inproj{layer}",
    )(h, w_in)


def _pool_consts(n, seg):
    t = np.arange(n)
    pos = t % seg
    base = t - pos
    mats = np.zeros((POOL_GROUPS, n, n), np.float32)
    cnts = np.zeros((POOL_GROUPS, n, 1), np.float32)
    for gi, w in enumerate(POOL_WINDOWS):
        lo = np.clip(pos - w // 2, 0, seg)
        hi = np.clip(pos + w - w // 2, 0, seg)
        s = t[None, :]
        mats[gi] = ((s >= (base + lo)[:, None]) & (s < (base + hi)[:, None])).astype(np.float32)
        cnts[gi, :, 0] = hi - lo
    return jnp.asarray(mats, BF16), jnp.asarray(cnts, F32)


def _pool(u_ref, band_ref, cnt_ref, pw_ref, ps_ref, op_ref):
    for gi in range(POOL_GROUPS):
        cols = slice(gi * POOL_GC, (gi + 1) * POOL_GC)
        ug = u_ref[:, cols]
        u1 = ug.astype(BF16)
        u2 = (ug - u1.astype(F32)).astype(BF16)
        u3 = (ug - u1.astype(F32) - u2.astype(F32)).astype(BF16)
        band = band_ref[gi]
        wsum = _dot(band, u1) + _dot(band, u2) + _dot(band, u3)
        d = (wsum / cnt_ref[gi] - ug).astype(BF16)
        y = _dot(d, pw_ref[0, gi].astype(BF16))
        op_ref[:, cols] = (y * ps_ref[0, :, cols]).astype(BF16)


def _softmax_av(s_parts, v_parts):
    m = s_parts[0].max(axis=-1, keepdims=True)
    for s in s_parts[1:]:
        m = jnp.maximum(m, s.max(axis=-1, keepdims=True))
    den = None
    acc = None
    for s, v in zip(s_parts, v_parts):
        p = jnp.exp(s - m)
        ps = p.sum(axis=-1, keepdims=True)
        den = ps if den is None else den + ps
        pv = _dot(p.astype(BF16), v)
        acc = pv if acc is None else acc + pv
    return acc / den


def _attend_pair(q, ks, vs, biases):
    m = q.shape[0]
    first = lax.broadcasted_iota(jnp.int32, (m, PAIR), 1) < NA_HEAD_DIM
    qf = q.astype(F32)
    q2 = jnp.concatenate([jnp.where(first, qf, 0.0), jnp.where(first, 0.0, qf)], axis=0).astype(BF16)
    scores = [_dot_nt(q2, k) if b is None else _dot_nt(q2, k) + b for k, b in zip(ks, biases)]
    o2 = _softmax_av(scores, vs)
    return jnp.where(first, o2[:m], o2[m:])


def _ctx_mixer_kernel(q_ref, k_ref, v_ref, u_ref, band_ref, cnt_ref, pw_ref, ps_ref, oa_ref, op_ref):
    for hp in range(NA_HEADS // 2):
        cols = slice(hp * PAIR, (hp + 1) * PAIR)
        o = _attend_pair(q_ref[:, cols], [k_ref[:, cols].astype(BF16)], [v_ref[:, cols].astype(BF16)], [None])
        oa_ref[:, cols] = o.astype(BF16)
    _pool(u_ref, band_ref, cnt_ref, pw_ref, ps_ref, op_ref)


def _ctx_mixer(q, kc, vc, u, pool_w, pool_scale, layer):
    band, cnt = _pool_consts(SEQ, SEQ)
    seq = lambda width: pl.BlockSpec((SEQ, width), lambda s: (s, 0))
    return pl.pallas_call(
        _ctx_mixer_kernel,
        grid=(BATCH,),
        in_specs=[
            seq(NA_WIDTH), seq(NA_WIDTH), seq(NA_WIDTH), seq(POOL_WIDTH),
            pl.BlockSpec((POOL_GROUPS, SEQ, SEQ), lambda s: (0, 0, 0)),
            pl.BlockSpec((POOL_GROUPS, SEQ, 1), lambda s: (0, 0, 0)),
            pl.BlockSpec((1, POOL_GROUPS, POOL_GC, POOL_GC), lambda s: (layer, 0, 0, 0)),
            pl.BlockSpec((1, 1, POOL_WIDTH), lambda s: (layer, 0, 0)),
        ],
        out_specs=[seq(NA_WIDTH), seq(POOL_WIDTH)],
        out_shape=[jax.ShapeDtypeStruct((T_CTX, NA_WIDTH), BF16),
                   jax.ShapeDtypeStruct((T_CTX, POOL_WIDTH), BF16)],
        compiler_params=_params(("arbitrary",)),
        name=f"ctx_mixer{layer}",
    )(q, kc, vc, u, band, cnt, pool_w, pool_scale.reshape(DEPTH, 1, POOL_WIDTH))


def _na_build_bias(rpb_ref, bias_ref, r, rs):
    lanes = 2 * GRID_W
    qc = lax.broadcasted_iota(jnp.int32, (GRID_W, lanes), 0)
    lane = lax.broadcasted_iota(jnp.int32, (GRID_W, lanes), 1)
    kc = jnp.bitwise_and(lane, GRID_W - 1)
    cstart = jnp.clip(qc - WIN_W // 2, 0, GRID_W - WIN_W)
    valid = (kc >= cstart) & (kc < cstart + WIN_W)
    for h in range(NA_HEADS):
        for jp in range(WIN_H // 2):
            halves = []
            for j, shift in ((2 * jp, lanes - (WIN_W - 1)), (2 * jp + 1, GRID_W - (WIN_W - 1))):
                dr = rs + j - r + (WIN_H - 1)
                row = jnp.broadcast_to(rpb_ref[0, h, pl.ds(dr, 1), :], (GRID_W, lanes))
                halves.append(pltpu.roll(row, shift, 1, stride=1, stride_axis=0))
            tile = jnp.where(lane < GRID_W, halves[0], halves[1])
            bias_ref[h, :, jp * lanes:(jp + 1) * lanes] = jnp.where(valid, tile, MASKED)


def _na_mixer_kernel(q_ref, k_ref, v_ref, ck_ref, cv_ref, rpb_ref, u_ref, band_ref, cnt_ref, pw_ref, ps_ref,
                     oa_ref, op_ref, bias_ref, ckb_ref, cvb_ref):
    r = pl.program_id(1)
    rs = jnp.clip(r - WIN_H // 2, 0, GRID_ROWS - WIN_H)

    @pl.when(r == 0)
    def _():
        ckb_ref[...] = ck_ref[0, 0].astype(BF16)
        cvb_ref[...] = cv_ref[0, 0].astype(BF16)

    @pl.when((r <= WIN_H // 2) | (r > GRID_ROWS - WIN_H // 2))
    def _():
        _na_build_bias(rpb_ref, bias_ref, r, rs)

    keys = pl.ds(pl.multiple_of(rs * GRID_W, GRID_W), NA_KEYS)
    for hp in range(NA_HEADS // 2):
        cols = slice(hp * PAIR, (hp + 1) * PAIR)
        bias = bias_ref[2 * hp:2 * hp + 2].reshape(2 * GRID_W, NA_KEYS)
        o = _attend_pair(q_ref[:, cols], [k_ref[keys, cols], ckb_ref[:, cols]],
                         [v_ref[keys, cols], cvb_ref[:, cols]], [bias, None])
        oa_ref[:, cols] = o.astype(BF16)
    _pool(u_ref, band_ref, cnt_ref, pw_ref, ps_ref, op_ref)


def _na_mixer(q, ks, vs, u, cache_k, cache_v, rpb_pad, pool_w, pool_scale, layer):
    band, cnt = _pool_consts(GRID_W, GRID_W)
    row0 = T_CTX // GRID_W
    row = lambda width: pl.BlockSpec((GRID_W, width), lambda b, r: (row0 + b * GRID_ROWS + r, 0))
    batch = pl.BlockSpec((DEC_SEQ, NA_WIDTH), lambda b, r: (b, 0))
    cblk = pl.BlockSpec((1, 1, PAST_LEN, NA_WIDTH), lambda b, r: (b, layer, 0, 0))
    out = lambda width: pl.BlockSpec((GRID_W, width), lambda b, r: (b * GRID_ROWS + r, 0))
    return pl.pallas_call(
        _na_mixer_kernel,
        grid=(DEC_BATCH, GRID_ROWS),
        in_specs=[
            row(NA_WIDTH), batch, batch, cblk, cblk,
            pl.BlockSpec((1,) + rpb_pad.shape[1:], lambda b, r: (layer, 0, 0, 0)),
            row(POOL_WIDTH),
            pl.BlockSpec((POOL_GROUPS, GRID_W, GRID_W), lambda b, r: (0, 0, 0)),
            pl.BlockSpec((POOL_GROUPS, GRID_W, 1), lambda b, r: (0, 0, 0)),
            pl.BlockSpec((1, POOL_GROUPS, POOL_GC, POOL_GC), lambda b, r: (layer, 0, 0, 0)),
            pl.BlockSpec((1, 1, POOL_WIDTH), lambda b, r: (layer, 0, 0)),
        ],
        out_specs=[out(NA_WIDTH), out(POOL_WIDTH)],
        out_shape=[jax.ShapeDtypeStruct((T_SMP, NA_WIDTH), BF16),
                   jax.ShapeDtypeStruct((T_SMP, POOL_WIDTH), BF16)],
        scratch_shapes=[pltpu.VMEM((NA_HEADS, GRID_W, NA_KEYS), F32),
                        pltpu.VMEM((PAST_LEN, NA_WIDTH), BF16), pltpu.VMEM((PAST_LEN, NA_WIDTH), BF16)],
        compiler_params=_params(("arbitrary", "arbitrary")),
        name=f"na_mixer{layer}",
    )(q, ks, vs,
      cache_k.reshape(DEC_BATCH, DEPTH, PAST_LEN, NA_WIDTH), cache_v.reshape(DEC_BATCH, DEPTH, PAST_LEN, NA_WIDTH),
      rpb_pad, u, band, cnt, pool_w, pool_scale.reshape(DEPTH, 1, POOL_WIDTH))


MERGE_TM = 512
MERGE_HALF = MERGE_TM // 2


def _halves():
    return [slice(k * MERGE_HALF, (k + 1) * MERGE_HALF) for k in range(2)]


def _branch_kernel(oac_ref, opc_ref, oas_ref, ops_ref, ga_ref, gp_ref, wba_ref, wbp_ref, mix_ref):
    is_ctx = pl.program_id(0) < T_CTX // MERGE_TM
    for rows in _halves():
        a = _dot(jnp.where(is_ctx, oac_ref[rows, :], oas_ref[rows, :]), wba_ref[0])
        p = _dot(jnp.where(is_ctx, opc_ref[rows, :], ops_ref[rows, :]), wbp_ref[0])
        mix = (jax.nn.sigmoid(ga_ref[rows, :].astype(F32)) * a
               + jax.nn.sigmoid(gp_ref[rows, :].astype(F32)) * p)
        mix_ref[rows, :] = mix.astype(BF16)


def _branch(oa_c, op_c, oa_s, op_s, gates, w_ba, w_bp, layer):
    tm = MERGE_TM
    full = lambda shape: pl.BlockSpec(shape, lambda i: (layer,) + (0,) * (len(shape) - 1))
    na, pool = _two_group_specs(tm, NA_WIDTH), _two_group_specs(tm, POOL_WIDTH)
    return pl.pallas_call(
        _branch_kernel,
        grid=(T_ALL // tm,),
        in_specs=[na[0], pool[0], na[1], pool[1],
                  pl.BlockSpec((tm, D_MODEL), lambda i: (i, 0)),
                  pl.BlockSpec((tm, D_MODEL), lambda i: (i, 1)),
                  full((1, NA_WIDTH, D_MODEL)),
                  full((1, POOL_WIDTH, D_MODEL))],
        out_specs=pl.BlockSpec((tm, D_MODEL), lambda i: (i, 0)),
        out_shape=jax.ShapeDtypeStruct((T_ALL, D_MODEL), BF16),
        compiler_params=_params(("arbitrary",)),
        name=f"branch{layer}",
    )(oa_c, op_c, oa_s, op_s, gates, gates, w_ba, w_bp)


def _merge_kernel(n_x, mix_ref, *refs):
    x_refs, (mod_ref, wout_ref, n2g_ref, rwt_ref, xo_ref, h2_ref, lg_ref) = refs[:n_x], refs[n_x:]
    is_ctx = pl.program_id(0) < T_CTX // MERGE_TM
    w1, w2 = _split2(rwt_ref[...])
    for rows in _halves():
        m = _dot(mix_ref[rows, :], wout_ref[0])
        x_in = x_refs[0][rows, :] if n_x == 1 else jnp.where(is_ctx, x_refs[0][rows, :], x_refs[1][rows, :])
        x = x_in + mod_ref[0, 0, G1:G1 + 1, :] * m
        xo_ref[rows, :] = x
        h2 = _mod_norm(x, n2g_ref[0], mod_ref, SH2, SC2)
        h2_ref[rows, :] = h2
        t1, t2 = _split2(h2)
        lg_ref[:, rows] = _dot_nt(w1, t1) + _dot_nt(w1, t2) + _dot_nt(w2, t1)


def _merge(mix, x_parts, mods, w_out, norm2_g, router_wt, layer):
    tm = MERGE_TM
    full = lambda shape: pl.BlockSpec(shape, lambda i: (layer,) + (0,) * (len(shape) - 1))
    tok = pl.BlockSpec((tm, D_MODEL), lambda i: (i, 0))
    x_specs = [tok] if len(x_parts) == 1 else _two_group_specs(tm, D_MODEL)
    return pl.pallas_call(
        functools.partial(_merge_kernel, len(x_parts)),
        grid=(T_ALL // tm,),
        in_specs=[tok] + x_specs + [
            _mod_spec(layer, tm),
            full((1, D_MODEL, D_MODEL)),
            full((1, 1, D_MODEL)),
            pl.BlockSpec((N_EXPERTS, D_MODEL), lambda i: (0, 0)),
        ],
        out_specs=[tok, tok, pl.BlockSpec((N_EXPERTS, tm), lambda i: (0, i))],
        out_shape=[jax.ShapeDtypeStruct((T_ALL, D_MODEL), F32),
                   jax.ShapeDtypeStruct((T_ALL, D_MODEL), F32),
                   jax.ShapeDtypeStruct((N_EXPERTS, T_ALL), F32)],
        compiler_params=_params(("arbitrary",)),
        name=f"merge{layer}",
    )(mix, *x_parts, mods, w_out, norm2_g.reshape(DEPTH, 1, D_MODEL), router_wt)


def _route_kernel(lg_ref, rb_ref, tri_ref, idx_ref, w_ref, rank_ref, cnt_ref, carry_ref):
    @pl.when(pl.program_id(0) == 0)
    def _():
        carry_ref[...] = jnp.zeros_like(carry_ref)

    scores = jax.nn.sigmoid(lg_ref[...])
    sel = scores + rb_ref[...]
    gscore = []
    for g in range(N_GROUPS):
        a, b, c, d = [sel[EXPERTS_PER_GROUP * g + i:EXPERTS_PER_GROUP * g + i + 1, :] for i in range(4)]
        hi1, lo1, hi2, lo2 = jnp.maximum(a, b), jnp.minimum(a, b), jnp.maximum(c, d), jnp.minimum(c, d)
        gscore.append(jnp.maximum(hi1, hi2) + jnp.maximum(jnp.minimum(hi1, hi2), jnp.maximum(lo1, lo2)))
    best = jnp.zeros_like(gscore[0], dtype=jnp.int32)
    bestv = gscore[0]
    for g in range(1, N_GROUPS):
        upd = gscore[g] > bestv
        best = jnp.where(upd, g, best)
        bestv = jnp.where(upd, gscore[g], bestv)
    row = lax.broadcasted_iota(jnp.int32, sel.shape, 0)
    masked = jnp.where(row // EXPERTS_PER_GROUP == best, sel, -jnp.inf)
    i0 = jnp.min(jnp.where(masked == masked.max(axis=0, keepdims=True), row, N_EXPERTS), axis=0, keepdims=True)
    masked = jnp.where(row == i0, -jnp.inf, masked)
    i1 = jnp.min(jnp.where(masked == masked.max(axis=0, keepdims=True), row, N_EXPERTS), axis=0, keepdims=True)
    s0 = jnp.sum(jnp.where(row == i0, scores, 0.0), axis=0, keepdims=True)
    s1 = jnp.sum(jnp.where(row == i1, scores, 0.0), axis=0, keepdims=True)
    den = s0 + s1
    idx_ref[...] = jnp.concatenate([i0, i1], axis=0)
    w_ref[...] = jnp.concatenate([s0 / den, s1 / den], axis=0)

    hit = jnp.where((row == i0) | (row == i1), 1.0, 0.0)
    before = _dot(hit.astype(BF16), tri_ref[...]) + carry_ref[:, 0:1]
    r0 = jnp.sum(jnp.where(row == i0, before, 0.0), axis=0, keepdims=True)
    r1 = jnp.sum(jnp.where(row == i1, before, 0.0), axis=0, keepdims=True)
    rank_ref[...] = jnp.concatenate([r0, r1], axis=0).astype(jnp.int32)
    carry_ref[...] = carry_ref[...] + jnp.sum(hit, axis=1, keepdims=True)
    cnt_ref[...] = carry_ref[...]


def _route(logits_t, router_bias):
    tn = ROUTE_BLOCK
    tri = jnp.asarray(np.triu(np.ones((tn, tn), np.float32), k=1), BF16)
    pair = lambda dt: jax.ShapeDtypeStruct((TOP_K, T_ALL), dt)
    return pl.pallas_call(
        _route_kernel,
        grid=(T_ALL // tn,),
        in_specs=[pl.BlockSpec((N_EXPERTS, tn), lambda i: (0, i)),
                  pl.BlockSpec((N_EXPERTS, 1), lambda i: (0, 0)),
                  pl.BlockSpec((tn, tn), lambda i: (0, 0))],
        out_specs=[pl.BlockSpec((TOP_K, tn), lambda i: (0, i)),
                   pl.BlockSpec((TOP_K, tn), lambda i: (0, i)),
                   pl.BlockSpec((TOP_K, tn), lambda i: (0, i)),
                   pl.BlockSpec((N_EXPERTS, LANES), lambda i: (0, 0))],
        out_shape=[pair(jnp.int32), pair(F32), pair(jnp.int32),
                   jax.ShapeDtypeStruct((N_EXPERTS, LANES), F32)],
        scratch_shapes=[pltpu.VMEM((N_EXPERTS, LANES), F32)],
        compiler_params=_params(("arbitrary",)),
        name="route",
    )(logits_t, router_bias.reshape(N_EXPERTS, 1).astype(F32), tri)


def _slot_kernel(idx_ref, rank_ref, off_ref, pos_ref):
    row = lax.broadcasted_iota(jnp.int32, (N_EXPERTS, idx_ref.shape[1]), 0)
    for j in range(TOP_K):
        off = jnp.sum(jnp.where(row == idx_ref[j:j + 1, :], off_ref[...], 0), axis=0, keepdims=True)
        pos_ref[j:j + 1, :] = off + rank_ref[j:j + 1, :]


def _slots(idx_t, rank_t, seg_start):
    tn = ROUTE_BLOCK
    blk = pl.BlockSpec((TOP_K, tn), lambda i: (0, i))
    return pl.pallas_call(
        _slot_kernel,
        grid=(T_ALL // tn,),
        in_specs=[blk, blk, pl.BlockSpec((N_EXPERTS, 1), lambda i: (0, 0))],
        out_specs=blk,
        out_shape=jax.ShapeDtypeStruct((TOP_K, T_ALL), jnp.int32),
        compiler_params=_params(("arbitrary",)),
        name="slots",
    )(idx_t, rank_t, seg_start.reshape(N_EXPERTS, 1))


def _tile_plan(counts):
    cnt = counts.astype(jnp.int32)
    ntile = (cnt + MOE_TM - 1) // MOE_TM
    tile_end = jnp.cumsum(ntile)
    seg_start = (tile_end - ntile) * MOE_TM
    n_used = tile_end[-1]
    tiles = jnp.arange(MOE_TILES, dtype=jnp.int32)
    live = jnp.minimum(tiles, n_used - 1)
    expert = jnp.sum((live[:, None] >= tile_end[None, :]).astype(jnp.int32), axis=1)
    first = ((live == (tile_end - ntile)[expert]) & (tiles < n_used)).astype(jnp.int32)
    pad_lo = jnp.concatenate([seg_start + cnt, (n_used * MOE_TM).reshape(1)])
    pad_hi = jnp.concatenate([tile_end * MOE_TM, jnp.full((1,), MOE_TILES * MOE_TM, jnp.int32)])
    return seg_start, expert, first, n_used.reshape(1), pad_lo, pad_hi


SUBLANES = 8


def _row_copy(src, i, dst, tile, sub, sem):
    return pltpu.make_async_copy(src.at[pl.ds(i, 1), :], dst.at[tile, pl.ds(sub, 1), :], sem)


def _gather_rows(src, index_of, dst, n_rows, sem):
    def body(tile, carry):
        for sub in range(SUBLANES):
            _row_copy(src, index_of(tile * SUBLANES + sub), dst, tile, sub, sem).start()
        return carry
    lax.fori_loop(0, n_rows // SUBLANES, body, 0)


def _wait_rows(src, dst, n_rows, sem):
    for _ in range(n_rows):
        _row_copy(src, 0, dst, 0, 0, sem).wait()


def _invert_kernel(pos0_ref, pos1_ref, pad_lo_ref, pad_hi_ref, src_ref):
    def fill(s, carry):
        src_ref[s] = 0
        return carry

    def put(t, carry):
        src_ref[pos0_ref[t]] = t
        src_ref[pos1_ref[t]] = t
        return carry

    for k in range(N_EXPERTS + 1):
        lax.fori_loop(pad_lo_ref[k], pad_hi_ref[k], fill, 0)
    lax.fori_loop(0, T_ALL, put, 0, unroll=16)


def _invert(pos, pad_lo, pad_hi):
    return pl.pallas_call(
        _invert_kernel,
        grid_spec=pltpu.PrefetchScalarGridSpec(
            num_scalar_prefetch=4,
            grid=(1,),
            in_specs=[],
            out_specs=pl.BlockSpec(memory_space=pltpu.SMEM),
        ),
        out_shape=jax.ShapeDtypeStruct((MOE_TILES * MOE_TM,), jnp.int32),
        compiler_params=_params(("arbitrary",)),
        name="invert",
    )(pos[0], pos[1], pad_lo, pad_hi)


def _moe_kernel(src_ref, expert_ref, first_ref, nused_ref, h_ref, wg_ref, wu_ref, wd_ref, y_ref,
                xa, xb, sems, wg_bf, wu_bf, wd_bf):
    i = pl.program_id(0)
    n_used = nused_ref[0]

    def gather(tile, buf, sem, unrolled):
        base = jnp.minimum(tile, MOE_TILES - 1) * MOE_TM
        if unrolled:
            for r in range(MOE_TM):
                _row_copy(h_ref, src_ref[base + r], buf, r // SUBLANES, r % SUBLANES, sem).start(priority=r % 2)
        else:
            _gather_rows(h_ref, lambda r: src_ref[base + r], buf, MOE_TM, sem)

    @pl.when(i == 0)
    def _():
        gather(0, xa, sems.at[0], False)
        gather(1, xb, sems.at[1], False)

    @pl.when((first_ref[jnp.minimum(i, MOE_TILES - 1)] == 1) & (i < MOE_TILES))
    def _():
        wg_bf[...] = wg_ref[0, 0].astype(BF16)
        wu_bf[...] = wu_ref[0, 0].astype(BF16)
        wd_bf[...] = wd_ref[0, 0].astype(BF16)

    for parity, buf in enumerate((xa, xb)):
        sem = sems.at[parity]
        mine = i % 2 == parity

        @pl.when((i < n_used) & mine)
        def _():
            _wait_rows(h_ref, buf, MOE_TM, sem)
            x = buf[...].reshape(MOE_TM, D_MODEL).astype(BF16)
            gather(i + 2, buf, sem, True)
            act = jax.nn.silu(_dot(x, wg_bf[...])) * _dot(x, wu_bf[...])
            y_ref[...] = _dot(act.astype(BF16), wd_bf[...])

        @pl.when((i >= n_used) & (i < n_used + 2) & mine)
        def _():
            _wait_rows(h_ref, buf, MOE_TM, sem)

    @pl.when((i >= n_used) & (i < MOE_TILES))
    def _():
        y_ref[...] = jnp.zeros_like(y_ref)


def _moe(h2, src, expert, first, n_used, w_gate, w_up, w_down, layer):
    tile = lambda i: jnp.minimum(i, MOE_TILES - 1)
    wspec = lambda shape: pl.BlockSpec((1, 1) + shape, lambda i, sr, ex, fi, nu: (layer, ex[tile(i)], 0, 0))
    xbuf = pltpu.VMEM((MOE_TM // SUBLANES, SUBLANES, D_MODEL), F32)
    return pl.pallas_call(
        _moe_kernel,
        grid_spec=pltpu.PrefetchScalarGridSpec(
            num_scalar_prefetch=4,
            grid=(MOE_TILES + 2,),
            in_specs=[pl.BlockSpec(memory_space=pl.ANY),
                      wspec((D_MODEL, EXPERT_FF)), wspec((D_MODEL, EXPERT_FF)), wspec((EXPERT_FF, D_MODEL))],
            out_specs=pl.BlockSpec((MOE_TM, D_MODEL), lambda i, sr, ex, fi, nu: (tile(i), 0)),
            scratch_shapes=[xbuf, xbuf, pltpu.SemaphoreType.DMA((2,)),
                            pltpu.VMEM((D_MODEL, EXPERT_FF), BF16), pltpu.VMEM((D_MODEL, EXPERT_FF), BF16),
                            pltpu.VMEM((EXPERT_FF, D_MODEL), BF16)],
        ),
        out_shape=jax.ShapeDtypeStruct((MOE_TILES * MOE_TM, D_MODEL), F32),
        compiler_params=_params(("arbitrary",), disable_bounds_checks=True),
        name=f"moe{layer}",
    )(src, expert, first, n_used, h2, w_gate, w_up, w_down)


def _combine_kernel(last, pos0_ref, pos1_ref, ys_ref, x_ref, w_ref, mod_ref, g_ref, nmod_ref, *refs):
    outs, (ya, yb, sems) = refs[:2], refs[2:]
    i = pl.program_id(0)
    tm = x_ref.shape[0]
    n = T_ALL // tm

    def gather(tile, buf, sem, unrolled):
        base = jnp.minimum(tile, n - 1) * tm
        for choice, pos_ref in enumerate((pos0_ref, pos1_ref)):
            if unrolled:
                for r in range(tm):
                    _row_copy(ys_ref, pos_ref[base + r], buf.at[choice], r // SUBLANES, r % SUBLANES,
                              sem).start(priority=r % 2)
            else:
                _gather_rows(ys_ref, lambda r: pos_ref[base + r], buf.at[choice], tm, sem)

    @pl.when(i == 0)
    def _():
        gather(0, ya, sems.at[0], False)
        gather(1, yb, sems.at[1], False)

    for parity, buf in enumerate((ya, yb)):
        sem = sems.at[parity]
        mine = i % 2 == parity

        @pl.when((i < n) & mine)
        def _():
            _wait_rows(ys_ref, buf.at[0], TOP_K * tm, sem)
            w0 = _row_to_col(w_ref[0:1, :])
            w1 = _row_to_col(w_ref[1:2, :])
            y = w0 * buf[0].reshape(tm, D_MODEL) + w1 * buf[1].reshape(tm, D_MODEL)
            gather(i + 2, buf, sem, True)
            x = x_ref[...] + mod_ref[0, 0, G2:G2 + 1, :] * y
            if last:
                yc_ref, ys_out_ref = outs
                final = _rms(x) * g_ref[0]

                @pl.when(i < T_CTX // tm)
                def _():
                    yc_ref[...] = final

                @pl.when(i >= T_CTX // tm)
                def _():
                    ys_out_ref[...] = final
            else:
                xo_ref, h_ref = outs
                xo_ref[...] = x
                h_ref[...] = _mod_norm(x, g_ref[0], nmod_ref, SH1, SC1).astype(BF16)

        @pl.when((i >= n) & mine)
        def _():
            _wait_rows(ys_ref, buf.at[0], TOP_K * tm, sem)


def _combine(ys, pos, w_t, x, mods, norm_g, layer):
    tm = COMBINE_TM
    n = T_ALL // tm
    last = layer == DEPTH - 1
    nxt = 0 if last else layer + 1
    tile = lambda i: jnp.minimum(i, n - 1)
    tok = pl.BlockSpec((tm, D_MODEL), lambda i, p0, p1: (tile(i), 0))
    if last:
        out_specs = _two_group_specs(tm, D_MODEL)
        out_shape = [jax.ShapeDtypeStruct((T_CTX, D_MODEL), F32), jax.ShapeDtypeStruct((T_SMP, D_MODEL), F32)]
    else:
        out_specs = [tok, tok]
        out_shape = [jax.ShapeDtypeStruct((T_ALL, D_MODEL), F32), jax.ShapeDtypeStruct((T_ALL, D_MODEL), BF16)]
    ybuf = pltpu.VMEM((TOP_K, tm // SUBLANES, SUBLANES, D_MODEL), F32)
    return pl.pallas_call(
        functools.partial(_combine_kernel, last),
        grid_spec=pltpu.PrefetchScalarGridSpec(
            num_scalar_prefetch=2,
            grid=(n + 2,),
            in_specs=[pl.BlockSpec(memory_space=pl.ANY),
                      tok,
                      pl.BlockSpec((TOP_K, tm), lambda i, p0, p1: (0, tile(i))),
                      _mod_spec(layer, tm),
                      pl.BlockSpec((1, 1, D_MODEL), lambda i, p0, p1: (nxt, 0, 0)),
                      _mod_spec(nxt, tm)],
            out_specs=out_specs,
            scratch_shapes=[ybuf, ybuf, pltpu.SemaphoreType.DMA((2,))],
        ),
        out_shape=out_shape,
        compiler_params=_params(("arbitrary",), disable_bounds_checks=True),
        name=f"combine{layer}",
    )(pos[0], pos[1], ys, x, w_t, mods, norm_g, mods)


def _stack_kernel(*refs):
    ins, outs = refs[:2 * DEPTH], refs[2 * DEPTH:]
    for t, out_ref in enumerate(outs):
        for l in range(DEPTH):
            out_ref[0, l] = ins[t * DEPTH + l][...]


def _stack_kv(ks, vs):
    seq = pl.BlockSpec((SEQ, NA_WIDTH), lambda b: (b, 0))
    out = pl.BlockSpec((1, DEPTH, SEQ, NA_WIDTH), lambda b: (b, 0, 0, 0))
    shape = jax.ShapeDtypeStruct((BATCH, DEPTH, SEQ, NA_WIDTH), F32)
    return pl.pallas_call(
        _stack_kernel,
        grid=(BATCH,),
        in_specs=[seq] * (2 * DEPTH),
        out_specs=[out, out],
        out_shape=[shape, shape],
        compiler_params=_params(("arbitrary",)),
        name="stack_kv",
    )(*ks, *vs)


def kernel(x_prompt, x_sample, c, cache_k, cache_v, c_ctx, ada_w, ada_b, norm1_g, w_in, rpb, pool_w, pool_scale,
           w_branch_a, w_branch_p, w_out, norm2_g, router_w, router_bias, moe_w_gate, moe_w_up, moe_w_down, final_g):
    x_ctx = x_prompt.reshape(T_CTX, D_MODEL)
    x_smp = x_sample.reshape(T_SMP, D_MODEL)
    cond = jnp.concatenate([c_ctx[None, :], c, jnp.zeros((N_COND - 1 - DEC_BATCH, D_MODEL), F32)], axis=0)
    mods = _ada(cond, ada_w, ada_b).reshape(DEPTH, N_COND, N_MOD, D_MODEL)
    rpb_pad = jnp.pad(rpb.astype(F32), ((0, 0), (0, 0), (0, 1), (0, LANES - rpb.shape[-1])))
    w_ba, w_bp, w_o = w_branch_a.astype(BF16), w_branch_p.astype(BF16), w_out.astype(BF16)
    router_wt = router_w.T
    norm1 = norm1_g.reshape(DEPTH, 1, D_MODEL)

    h = _pre(x_ctx, x_smp, norm1_g, mods)
    x_parts = (x_ctx, x_smp)
    new_ks, new_vs = [], []
    for l in range(DEPTH):
        q, kc, ks, vc, vs, u, gates = _inproj(h, w_in, l)
        new_ks.append(kc)
        new_vs.append(vc)
        oa_c, op_c = _ctx_mixer(q, kc, vc, u, pool_w, pool_scale, l)
        oa_s, op_s = _na_mixer(q, ks, vs, u, cache_k, cache_v, rpb_pad, pool_w, pool_scale, l)
        mix = _branch(oa_c, op_c, oa_s, op_s, gates, w_ba, w_bp, l)
        x, h2, logits_t = _merge(mix, x_parts, mods, w_o, norm2_g, router_wt, l)
        idx_t, w_t, rank_t, counts = _route(logits_t, router_bias)
        seg_start, expert, first, n_used, pad_lo, pad_hi = _tile_plan(counts[:, 0])
        pos = _slots(idx_t, rank_t, seg_start)
        src = _invert(pos, pad_lo, pad_hi)
        ys = _moe(h2, src, expert, first, n_used, moe_w_gate, moe_w_up, moe_w_down, l)
        if l < DEPTH - 1:
            x, h = _combine(ys, pos, w_t, x, mods, norm1, l)
            x_parts = (x,)
        else:
            y_ctx, y_smp = _combine(ys, pos, w_t, x, mods, final_g.reshape(1, 1, D_MODEL), l)
    new_k, new_v = _stack_kv(new_ks, new_vs)
    heads = (BATCH, DEPTH, SEQ, NA_HEADS, NA_HEAD_DIM)
    return (y_ctx.reshape(BATCH, SEQ, D_MODEL), y_smp.reshape(DEC_BATCH, DEC_SEQ, D_MODEL),
            new_k.reshape(heads), new_v.reshape(heads))
```

```python
import functools

import numpy as np
import jax
import jax.numpy as jnp
from jax import lax
from jax.experimental import pallas as pl
from jax.experimental.pallas import tpu as pltpu

D_MODEL = 2048
BATCH = 32
SEQ = 256
DEPTH = 2
DEC_BATCH = 2
DEC_SEQ = 2048
PAST_LEN = 256
GRID_W = 64
GRID_ROWS = DEC_SEQ // GRID_W
NA_HEADS = 16
NA_HEAD_DIM = 64
NA_WIDTH = NA_HEADS * NA_HEAD_DIM
WIN_H = 8
WIN_W = 16
POOL_WIDTH = D_MODEL // 2
POOL_WINDOWS = (2, 4, 8, 16)
POOL_GROUPS = len(POOL_WINDOWS)
POOL_GC = POOL_WIDTH // POOL_GROUPS
IN_WIDTH = 3 * NA_WIDTH + POOL_WIDTH + 2 * D_MODEL
N_EXPERTS = 16
N_GROUPS = 4
EXPERTS_PER_GROUP = N_EXPERTS // N_GROUPS
TOP_K = 2
EXPERT_FF = D_MODEL // 4
N_MOD = 6
EPS = 1e-6

T_CTX = BATCH * SEQ
T_SMP = DEC_BATCH * DEC_SEQ
T_ALL = T_CTX + T_SMP
N_COND = 8
NA_KEYS = WIN_H * GRID_W
MASKED = -1e30

F32 = jnp.float32
BF16 = jnp.bfloat16
VMEM_LIMIT = 56 * 1024 * 1024

SH1, SC1, G1, SH2, SC2, G2 = range(6)
LANES = 128
PAIR = 2 * NA_HEAD_DIM
ROUTE_BLOCK = 512
MOE_TM = 512
MOE_TILES = T_ALL * TOP_K // MOE_TM + N_EXPERTS
COMBINE_TM = 256


def _params(sem, vmem=VMEM_LIMIT, **kw):
    return pltpu.CompilerParams(dimension_semantics=sem, vmem_limit_bytes=vmem, **kw)


def _cond_of_tile(i, tm):
    t0 = i * tm
    return jnp.where(t0 < T_CTX, 0, 1 + (t0 - T_CTX) // DEC_SEQ)


def _mod_spec(layer, tm):
    last = T_ALL // tm - 1
    return pl.BlockSpec((1, 1, N_MOD, D_MODEL),
                        lambda i, *_: (layer, _cond_of_tile(jnp.minimum(i, last), tm), 0, 0))


def _two_group_specs(tm, width):
    nc = T_CTX // tm
    ns = T_SMP // tm
    return [pl.BlockSpec((tm, width), lambda i, *_: (jnp.minimum(i, nc - 1), 0)),
            pl.BlockSpec((tm, width), lambda i, *_: (jnp.clip(i - nc, 0, ns - 1), 0))]


def _rms(x):
    return x * lax.rsqrt(jnp.mean(x * x, axis=-1, keepdims=True) + EPS)


def _mod_norm(x, g, mod_ref, shift_row, scale_row):
    return (_rms(x) * g * (1.0 + mod_ref[0, 0, scale_row:scale_row + 1, :])
            + mod_ref[0, 0, shift_row:shift_row + 1, :])


def _dot(a, b):
    return jnp.dot(a, b, preferred_element_type=F32)


def _dot_nt(a, b):
    return lax.dot_general(a, b, (((1,), (1,)), ((), ())), preferred_element_type=F32)


def _split2(x):
    x1 = x.astype(BF16)
    return x1, (x - x1.astype(F32)).astype(BF16)


def _row_to_col(v):
    n = v.shape[1]
    r = lax.broadcasted_iota(jnp.int32, (n, n), 0)
    c = lax.broadcasted_iota(jnp.int32, (n, n), 1)
    return jnp.sum(jnp.where(r == c, v, 0.0), axis=1, keepdims=True)


def _ada_kernel(c_ref, w_ref, b_ref, o_ref):
    s = jax.nn.silu(c_ref[...]).astype(BF16)
    o_ref[0] = _dot(s, w_ref[0].astype(BF16)) + b_ref[0]


def _ada(cond, ada_w, ada_b):
    tn = 1024
    nj = N_MOD * D_MODEL // tn
    return pl.pallas_call(
        _ada_kernel,
        grid=(DEPTH, nj),
        in_specs=[
            pl.BlockSpec((N_COND, D_MODEL), lambda l, j: (0, 0)),
            pl.BlockSpec((1, D_MODEL, tn), lambda l, j: (l, 0, j)),
            pl.BlockSpec((1, 1, tn), lambda l, j: (l, 0, j)),
        ],
        out_specs=pl.BlockSpec((1, N_COND, tn), lambda l, j: (l, 0, j)),
        out_shape=jax.ShapeDtypeStruct((DEPTH, N_COND, N_MOD * D_MODEL), F32),
        compiler_params=_params(("arbitrary", "arbitrary")),
        name="ada",
    )(cond, ada_w, ada_b.reshape(DEPTH, 1, N_MOD * D_MODEL))


def _pre_kernel(xc_ref, xs_ref, g_ref, mod_ref, h_ref):
    is_ctx = pl.program_id(0) < T_CTX // h_ref.shape[0]
    x = jnp.where(is_ctx, xc_ref[...], xs_ref[...])
    h_ref[...] = _mod_norm(x, g_ref[0], mod_ref, SH1, SC1).astype(BF16)


def _pre(x_ctx, x_smp, norm_g, mods):
    tm = 512
    return pl.pallas_call(
        _pre_kernel,
        grid=(T_ALL // tm,),
        in_specs=_two_group_specs(tm, D_MODEL) + [
            pl.BlockSpec((1, 1, D_MODEL), lambda i: (0, 0, 0)),
            _mod_spec(0, tm),
        ],
        out_specs=pl.BlockSpec((tm, D_MODEL), lambda i: (i, 0)),
        out_shape=jax.ShapeDtypeStruct((T_ALL, D_MODEL), BF16),
        compiler_params=_params(("arbitrary",)),
        name="pre",
    )(x_ctx, x_smp, norm_g.reshape(DEPTH, 1, D_MODEL), mods)


INPROJ_TM = 512
INPROJ_TN = 1024
_NI = T_ALL // INPROJ_TM
_NC = T_CTX // INPROJ_TM
_COLS = {name: (start // INPROJ_TN, width // INPROJ_TN) for name, start, width in (
    ("q", 0, NA_WIDTH), ("k", NA_WIDTH, NA_WIDTH), ("v", 2 * NA_WIDTH, NA_WIDTH),
    ("u", 3 * NA_WIDTH, POOL_WIDTH), ("g", 3 * NA_WIDTH + POOL_WIDTH, 2 * D_MODEL))}


def _visit(name, rows, last_row):
    j0, nj = _COLS[name]

    def index(j, i):
        row = jnp.where(j < j0, 0, jnp.where(j < j0 + nj, rows(i), last_row))
        return row, jnp.clip(j - j0, 0, nj - 1)
    return index


def _inproj_kernel(h_ref, w_ref, q_ref, kc_ref, ks_ref, vc_ref, vs_ref, u_ref, g_ref, wbf_ref):
    j = pl.program_id(0)
    i = pl.program_id(1)

    @pl.when(i == 0)
    def _():
        wbf_ref[...] = w_ref[0].astype(BF16)

    def project():
        return _dot(h_ref[...], wbf_ref[...])

    def during(name):
        j0, nj = _COLS[name]
        return (j >= j0) & (j < j0 + nj)

    @pl.when(during("q"))
    def _():
        q_ref[...] = (project() * NA_HEAD_DIM ** -0.5).astype(BF16)

    for name, ctx_ref, smp_ref in (("k", kc_ref, ks_ref), ("v", vc_ref, vs_ref)):
        @pl.when(during(name) & (i < _NC))
        def _():
            ctx_ref[...] = project()

        @pl.when(during(name) & (i >= _NC))
        def _():
            smp_ref[...] = project().astype(BF16)

    @pl.when(during("u"))
    def _():
        u_ref[...] = project()

    @pl.when(during("g"))
    def _():
        g_ref[...] = project().astype(BF16)


def _inproj(h, w_in, layer):
    tm, tn = INPROJ_TM, INPROJ_TN
    blk = pl.BlockSpec
    every = lambda i: i
    ctx_rows = lambda i: jnp.minimum(i, _NC - 1)
    smp_rows = lambda i: jnp.clip(i - _NC, 0, _NI - _NC - 1)
    sds = jax.ShapeDtypeStruct
    return pl.pallas_call(
        _inproj_kernel,
        grid=(IN_WIDTH // tn, _NI),
        in_specs=[blk((tm, D_MODEL), lambda j, i: (i, 0)),
                  blk((1, D_MODEL, tn), lambda j, i: (layer, 0, j))],
        out_specs=[blk((tm, tn), _visit("q", every, _NI - 1)),
                   blk((tm, tn), _visit("k", ctx_rows, _NC - 1)),
                   blk((tm, tn), _visit("k", smp_rows, _NI - _NC - 1)),
                   blk((tm, tn), _visit("v", ctx_rows, _NC - 1)),
                   blk((tm, tn), _visit("v", smp_rows, _NI - _NC - 1)),
                   blk((tm, tn), _visit("u", every, _NI - 1)),
                   blk((tm, tn), _visit("g", every, _NI - 1))],
        out_shape=[sds((T_ALL, NA_WIDTH), BF16),
                   sds((T_CTX, NA_WIDTH), F32), sds((T_SMP, NA_WIDTH), BF16),
                   sds((T_CTX, NA_WIDTH), F32), sds((T_SMP, NA_WIDTH), BF16),
                   sds((T_ALL, POOL_WIDTH), F32),
                   sds((T_ALL, 2 * D_MODEL), BF16)],
        scratch_shapes=[pltpu.VMEM((D_MODEL, tn), BF16)],
        compiler_params=_params(("arbitrary", "arbitrary")),
        name=f"inproj{layer}",
    )(h, w_in)


def _pool_consts(n, seg):
    t = np.arange(n)
    pos = t % seg
    base = t - pos
    mats = np.zeros((POOL_GROUPS, n, n), np.float32)
    cnts = np.zeros((POOL_GROUPS, n, 1), np.float32)
    for gi, w in enumerate(POOL_WINDOWS):
        lo = np.clip(pos - w // 2, 0, seg)
        hi = np.clip(pos + w - w // 2, 0, seg)
        s = t[None, :]
        mats[gi] = ((s >= (base + lo)[:, None]) & (s < (base + hi)[:, None])).astype(np.float32)
        cnts[gi, :, 0] = hi - lo
    return jnp.asarray(mats, BF16), jnp.asarray(cnts, F32)


def _pool(u_ref, band_ref, cnt_ref, pw_ref, ps_ref, op_ref):
    for gi in range(POOL_GROUPS):
        cols = slice(gi * POOL_GC, (gi + 1) * POOL_GC)
        ug = u_ref[:, cols]
        u1 = ug.astype(BF16)
        u2 = (ug - u1.astype(F32)).astype(BF16)
        u3 = (ug - u1.astype(F32) - u2.astype(F32)).astype(BF16)
        band = band_ref[gi]
        wsum = _dot(band, u1) + _dot(band, u2) + _dot(band, u3)
        d = (wsum / cnt_ref[gi] - ug).astype(BF16)
        y = _dot(d, pw_ref[0, gi].astype(BF16))
        op_ref[:, cols] = (y * ps_ref[0, :, cols]).astype(BF16)


def _softmax_av(s_parts, v_parts):
    m = s_parts[0].max(axis=-1, keepdims=True)
    for s in s_parts[1:]:
        m = jnp.maximum(m, s.max(axis=-1, keepdims=True))
    den = None
    acc = None
    for s, v in zip(s_parts, v_parts):
        p = jnp.exp(s - m)
        ps = p.sum(axis=-1, keepdims=True)
        den = ps if den is None else den + ps
        pv = _dot(p.astype(BF16), v)
        acc = pv if acc is None else acc + pv
    return acc / den


def _attend_pair(q, ks, vs, biases):
    m = q.shape[0]
    first = lax.broadcasted_iota(jnp.int32, (m, PAIR), 1) < NA_HEAD_DIM
    qf = q.astype(F32)
    q2 = jnp.concatenate([jnp.where(first, qf, 0.0), jnp.where(first, 0.0, qf)], axis=0).astype(BF16)
    scores = [_dot_nt(q2, k) if b is None else _dot_nt(q2, k) + b for k, b in zip(ks, biases)]
    o2 = _softmax_av(scores, vs)
    return jnp.where(first, o2[:m], o2[m:])


def _ctx_mixer_kernel(q_ref, k_ref, v_ref, u_ref, band_ref, cnt_ref, pw_ref, ps_ref, oa_ref, op_ref):
    for hp in range(NA_HEADS // 2):
        cols = slice(hp * PAIR, (hp + 1) * PAIR)
        o = _attend_pair(q_ref[:, cols], [k_ref[:, cols].astype(BF16)], [v_ref[:, cols].astype(BF16)], [None])
        oa_ref[:, cols] = o.astype(BF16)
    _pool(u_ref, band_ref, cnt_ref, pw_ref, ps_ref, op_ref)


def _ctx_mixer(q, kc, vc, u, pool_w, pool_scale, layer):
    band, cnt = _pool_consts(SEQ, SEQ)
    seq = lambda width: pl.BlockSpec((SEQ, width), lambda s: (s, 0))
    return pl.pallas_call(
        _ctx_mixer_kernel,
        grid=(BATCH,),
        in_specs=[
            seq(NA_WIDTH), seq(NA_WIDTH), seq(NA_WIDTH), seq(POOL_WIDTH),
            pl.BlockSpec((POOL_GROUPS, SEQ, SEQ), lambda s: (0, 0, 0)),
            pl.BlockSpec((POOL_GROUPS, SEQ, 1), lambda s: (0, 0, 0)),
            pl.BlockSpec((1, POOL_GROUPS, POOL_GC, POOL_GC), lambda s: (layer, 0, 0, 0)),
            pl.BlockSpec((1, 1, POOL_WIDTH), lambda s: (layer, 0, 0)),
        ],
        out_specs=[seq(NA_WIDTH), seq(POOL_WIDTH)],
        out_shape=[jax.ShapeDtypeStruct((T_CTX, NA_WIDTH), BF16),
                   jax.ShapeDtypeStruct((T_CTX, POOL_WIDTH), BF16)],
        compiler_params=_params(("arbitrary",)),
        name=f"ctx_mixer{layer}",
    )(q, kc, vc, u, band, cnt, pool_w, pool_scale.reshape(DEPTH, 1, POOL_WIDTH))


def _na_build_bias(rpb_ref, bias_ref, r, rs):
    lanes = 2 * GRID_W
    qc = lax.broadcasted_iota(jnp.int32, (GRID_W, lanes), 0)
    lane = lax.broadcasted_iota(jnp.int32, (GRID_W, lanes), 1)
    kc = jnp.bitwise_and(lane, GRID_W - 1)
    cstart = jnp.clip(qc - WIN_W // 2, 0, GRID_W - WIN_W)
    valid = (kc >= cstart) & (kc < cstart + WIN_W)
    for h in range(NA_HEADS):
        for jp in range(WIN_H // 2):
            halves = []
            for j, shift in ((2 * jp, lanes - (WIN_W - 1)), (2 * jp + 1, GRID_W - (WIN_W - 1))):
                dr = rs + j - r + (WIN_H - 1)
                row = jnp.broadcast_to(rpb_ref[0, h, pl.ds(dr, 1), :], (GRID_W, lanes))
                halves.append(pltpu.roll(row, shift, 1, stride=1, stride_axis=0))
            tile = jnp.where(lane < GRID_W, halves[0], halves[1])
            bias_ref[h, :, jp * lanes:(jp + 1) * lanes] = jnp.where(valid, tile, MASKED)


def _na_mixer_kernel(q_ref, k_ref, v_ref, ck_ref, cv_ref, rpb_ref, u_ref, band_ref, cnt_ref, pw_ref, ps_ref,
                     oa_ref, op_ref, bias_ref, ckb_ref, cvb_ref):
    r = pl.program_id(1)
    rs = jnp.clip(r - WIN_H // 2, 0, GRID_ROWS - WIN_H)

    @pl.when(r == 0)
    def _():
        ckb_ref[...] = ck_ref[0, 0].astype(BF16)
        cvb_ref[...] = cv_ref[0, 0].astype(BF16)

    @pl.when((r <= WIN_H // 2) | (r > GRID_ROWS - WIN_H // 2))
    def _():
        _na_build_bias(rpb_ref, bias_ref, r, rs)

    keys = pl.ds(pl.multiple_of(rs * GRID_W, GRID_W), NA_KEYS)
    for hp in range(NA_HEADS // 2):
        cols = slice(hp * PAIR, (hp + 1) * PAIR)
        bias = bias_ref[2 * hp:2 * hp + 2].reshape(2 * GRID_W, NA_KEYS)
        o = _attend_pair(q_ref[:, cols], [k_ref[keys, cols], ckb_ref[:, cols]],
                         [v_ref[keys, cols], cvb_ref[:, cols]], [bias, None])
        oa_ref[:, cols] = o.astype(BF16)
    _pool(u_ref, band_ref, cnt_ref, pw_ref, ps_ref, op_ref)


def _na_mixer(q, ks, vs, u, cache_k, cache_v, rpb_pad, pool_w, pool_scale, layer):
    band, cnt = _pool_consts(GRID_W, GRID_W)
    row0 = T_CTX // GRID_W
    row = lambda width: pl.BlockSpec((GRID_W, width), lambda b, r: (row0 + b * GRID_ROWS + r, 0))
    batch = pl.BlockSpec((DEC_SEQ, NA_WIDTH), lambda b, r: (b, 0))
    cblk = pl.BlockSpec((1, 1, PAST_LEN, NA_WIDTH), lambda b, r: (b, layer, 0, 0))
    out = lambda width: pl.BlockSpec((GRID_W, width), lambda b, r: (b * GRID_ROWS + r, 0))
    return pl.pallas_call(
        _na_mixer_kernel,
        grid=(DEC_BATCH, GRID_ROWS),
        in_specs=[
            row(NA_WIDTH), batch, batch, cblk, cblk,
            pl.BlockSpec((1,) + rpb_pad.shape[1:], lambda b, r: (layer, 0, 0, 0)),
            row(POOL_WIDTH),
            pl.BlockSpec((POOL_GROUPS, GRID_W, GRID_W), lambda b, r: (0, 0, 0)),
            pl.BlockSpec((POOL_GROUPS, GRID_W, 1), lambda b, r: (0, 0, 0)),
            pl.BlockSpec((1, POOL_GROUPS, POOL_GC, POOL_GC), lambda b, r: (layer, 0, 0, 0)),
            pl.BlockSpec((1, 1, POOL_WIDTH), lambda b, r: (layer, 0, 0)),
        ],
        out_specs=[out(NA_WIDTH), out(POOL_WIDTH)],
        out_shape=[jax.ShapeDtypeStruct((T_SMP, NA_WIDTH), BF16),
                   jax.ShapeDtypeStruct((T_SMP, POOL_WIDTH), BF16)],
        scratch_shapes=[pltpu.VMEM((NA_HEADS, GRID_W, NA_KEYS), F32),
                        pltpu.VMEM((PAST_LEN, NA_WIDTH), BF16), pltpu.VMEM((PAST_LEN, NA_WIDTH), BF16)],
        compiler_params=_params(("arbitrary", "arbitrary")),
        name=f"na_mixer{layer}",
    )(q, ks, vs,
      cache_k.reshape(DEC_BATCH, DEPTH, PAST_LEN, NA_WIDTH), cache_v.reshape(DEC_BATCH, DEPTH, PAST_LEN, NA_WIDTH),
      rpb_pad, u, band, cnt, pool_w, pool_scale.reshape(DEPTH, 1, POOL_WIDTH))


MERGE_TM = 512
MERGE_HALF = MERGE_TM // 2


def _halves():
    return [slice(k * MERGE_HALF, (k + 1) * MERGE_HALF) for k in range(2)]


def _branch_kernel(oac_ref, opc_ref, oas_ref, ops_ref, ga_ref, gp_ref, wba_ref, wbp_ref, mix_ref):
    is_ctx = pl.program_id(0) < T_CTX // MERGE_TM
    for rows in _halves():
        a = _dot(jnp.where(is_ctx, oac_ref[rows, :], oas_ref[rows, :]), wba_ref[0])
        p = _dot(jnp.where(is_ctx, opc_ref[rows, :], ops_ref[rows, :]), wbp_ref[0])
        mix = (jax.nn.sigmoid(ga_ref[rows, :].astype(F32)) * a
               + jax.nn.sigmoid(gp_ref[rows, :].astype(F32)) * p)
        mix_ref[rows, :] = mix.astype(BF16)


def _branch(oa_c, op_c, oa_s, op_s, gates, w_ba, w_bp, layer):
    tm = MERGE_TM
    full = lambda shape: pl.BlockSpec(shape, lambda i: (layer,) + (0,) * (len(shape) - 1))
    na, pool = _two_group_specs(tm, NA_WIDTH), _two_group_specs(tm, POOL_WIDTH)
    return pl.pallas_call(
        _branch_kernel,
        grid=(T_ALL // tm,),
        in_specs=[na[0], pool[0], na[1], pool[1],
                  pl.BlockSpec((tm, D_MODEL), lambda i: (i, 0)),
                  pl.BlockSpec((tm, D_MODEL), lambda i: (i, 1)),
                  full((1, NA_WIDTH, D_MODEL)),
                  full((1, POOL_WIDTH, D_MODEL))],
        out_specs=pl.BlockSpec((tm, D_MODEL), lambda i: (i, 0)),
        out_shape=jax.ShapeDtypeStruct((T_ALL, D_MODEL), BF16),
        compiler_params=_params(("arbitrary",)),
        name=f"branch{layer}",
    )(oa_c, op_c, oa_s, op_s, gates, gates, w_ba, w_bp)


def _merge_kernel(n_x, mix_ref, *refs):
    x_refs, (mod_ref, wout_ref, n2g_ref, rwt_ref, xo_ref, h2_ref, lg_ref) = refs[:n_x], refs[n_x:]
    is_ctx = pl.program_id(0) < T_CTX // MERGE_TM
    w1, w2 = _split2(rwt_ref[...])
    for rows in _halves():
        m = _dot(mix_ref[rows, :], wout_ref[0])
        x_in = x_refs[0][rows, :] if n_x == 1 else jnp.where(is_ctx, x_refs[0][rows, :], x_refs[1][rows, :])
        x = x_in + mod_ref[0, 0, G1:G1 + 1, :] * m
        xo_ref[rows, :] = x
        h2 = _mod_norm(x, n2g_ref[0], mod_ref, SH2, SC2)
        h2_ref[rows, :] = h2
        t1, t2 = _split2(h2)
        lg_ref[:, rows] = _dot_nt(w1, t1) + _dot_nt(w1, t2) + _dot_nt(w2, t1)


def _merge(mix, x_parts, mods, w_out, norm2_g, router_wt, layer):
    tm = MERGE_TM
    full = lambda shape: pl.BlockSpec(shape, lambda i: (layer,) + (0,) * (len(shape) - 1))
    tok = pl.BlockSpec((tm, D_MODEL), lambda i: (i, 0))
    x_specs = [tok] if len(x_parts) == 1 else _two_group_specs(tm, D_MODEL)
    return pl.pallas_call(
        functools.partial(_merge_kernel, len(x_parts)),
        grid=(T_ALL // tm,),
        in_specs=[tok] + x_specs + [
            _mod_spec(layer, tm),
            full((1, D_MODEL, D_MODEL)),
            full((1, 1, D_MODEL)),
            pl.BlockSpec((N_EXPERTS, D_MODEL), lambda i: (0, 0)),
        ],
        out_specs=[tok, tok, pl.BlockSpec((N_EXPERTS, tm), lambda i: (0, i))],
        out_shape=[jax.ShapeDtypeStruct((T_ALL, D_MODEL), F32),
                   jax.ShapeDtypeStruct((T_ALL, D_MODEL), F32),
                   jax.ShapeDtypeStruct((N_EXPERTS, T_ALL), F32)],
        compiler_params=_params(("arbitrary",)),
        name=f"merge{layer}",
    )(mix, *x_parts, mods, w_out, norm2_g.reshape(DEPTH, 1, D_MODEL), router_wt)


def _route_kernel(lg_ref, rb_ref, tri_ref, idx_ref, w_ref, rank_ref, cnt_ref, carry_ref):
    @pl.when(pl.program_id(0) == 0)
    def _():
        carry_ref[...] = jnp.zeros_like(carry_ref)

    scores = jax.nn.sigmoid(lg_ref[...])
    sel = scores + rb_ref[...]
    gscore = []
    for g in range(N_GROUPS):
        a, b, c, d = [sel[EXPERTS_PER_GROUP * g + i:EXPERTS_PER_GROUP * g + i + 1, :] for i in range(4)]
        hi1, lo1, hi2, lo2 = jnp.maximum(a, b), jnp.minimum(a, b), jnp.maximum(c, d), jnp.minimum(c, d)
        gscore.append(jnp.maximum(hi1, hi2) + jnp.maximum(jnp.minimum(hi1, hi2), jnp.maximum(lo1, lo2)))
    best = jnp.zeros_like(gscore[0], dtype=jnp.int32)
    bestv = gscore[0]
    for g in range(1, N_GROUPS):
        upd = gscore[g] > bestv
        best = jnp.where(upd, g, best)
        bestv = jnp.where(upd, gscore[g], bestv)
    row = lax.broadcasted_iota(jnp.int32, sel.shape, 0)
    masked = jnp.where(row // EXPERTS_PER_GROUP == best, sel, -jnp.inf)
    i0 = jnp.min(jnp.where(masked == masked.max(axis=0, keepdims=True), row, N_EXPERTS), axis=0, keepdims=True)
    masked = jnp.where(row == i0, -jnp.inf, masked)
    i1 = jnp.min(jnp.where(masked == masked.max(axis=0, keepdims=True), row, N_EXPERTS), axis=0, keepdims=True)
    s0 = jnp.sum(jnp.where(row == i0, scores, 0.0), axis=0, keepdims=True)
    s1 = jnp.sum(jnp.where(row == i1, scores, 0.0), axis=0, keepdims=True)
    den = s0 + s1
    idx_ref[...] = jnp.concatenate([i0, i1], axis=0)
    w_ref[...] = jnp.concatenate([s0 / den, s1 / den], axis=0)

    hit = jnp.where((row == i0) | (row == i1), 1.0, 0.0)
    before = _dot(hit.astype(BF16), tri_ref[...]) + carry_ref[:, 0:1]
    r0 = jnp.sum(jnp.where(row == i0, before, 0.0), axis=0, keepdims=True)
    r1 = jnp.sum(jnp.where(row == i1, before, 0.0), axis=0, keepdims=True)
    rank_ref[...] = jnp.concatenate([r0, r1], axis=0).astype(jnp.int32)
    carry_ref[...] = carry_ref[...] + jnp.sum(hit, axis=1, keepdims=True)
    cnt_ref[...] = carry_ref[...]


def _route(logits_t, router_bias):
    tn = ROUTE_BLOCK
    tri = jnp.asarray(np.triu(np.ones((tn, tn), np.float32), k=1), BF16)
    pair = lambda dt: jax.ShapeDtypeStruct((TOP_K, T_ALL), dt)
    return pl.pallas_call(
        _route_kernel,
        grid=(T_ALL // tn,),
        in_specs=[pl.BlockSpec((N_EXPERTS, tn), lambda i: (0, i)),
                  pl.BlockSpec((N_EXPERTS, 1), lambda i: (0, 0)),
                  pl.BlockSpec((tn, tn), lambda i: (0, 0))],
        out_specs=[pl.BlockSpec((TOP_K, tn), lambda i: (0, i)),
                   pl.BlockSpec((TOP_K, tn), lambda i: (0, i)),
                   pl.BlockSpec((TOP_K, tn), lambda i: (0, i)),
                   pl.BlockSpec((N_EXPERTS, LANES), lambda i: (0, 0))],
        out_shape=[pair(jnp.int32), pair(F32), pair(jnp.int32),
                   jax.ShapeDtypeStruct((N_EXPERTS, LANES), F32)],
        scratch_shapes=[pltpu.VMEM((N_EXPERTS, LANES), F32)],
        compiler_params=_params(("arbitrary",)),
        name="route",
    )(logits_t, router_bias.reshape(N_EXPERTS, 1).astype(F32), tri)


def _slot_kernel(idx_ref, rank_ref, off_ref, pos_ref):
    row = lax.broadcasted_iota(jnp.int32, (N_EXPERTS, idx_ref.shape[1]), 0)
    for j in range(TOP_K):
        off = jnp.sum(jnp.where(row == idx_ref[j:j + 1, :], off_ref[...], 0), axis=0, keepdims=True)
        pos_ref[j:j + 1, :] = off + rank_ref[j:j + 1, :]


def _slots(idx_t, rank_t, seg_start):
    tn = ROUTE_BLOCK
    blk = pl.BlockSpec((TOP_K, tn), lambda i: (0, i))
    return pl.pallas_call(
        _slot_kernel,
        grid=(T_ALL // tn,),
        in_specs=[blk, blk, pl.BlockSpec((N_EXPERTS, 1), lambda i: (0, 0))],
        out_specs=blk,
        out_shape=jax.ShapeDtypeStruct((TOP_K, T_ALL), jnp.int32),
        compiler_params=_params(("arbitrary",)),
        name="slots",
    )(idx_t, rank_t, seg_start.reshape(N_EXPERTS, 1))


def _tile_plan(counts):
    cnt = counts.astype(jnp.int32)
    ntile = (cnt + MOE_TM - 1) // MOE_TM
    tile_end = jnp.cumsum(ntile)
    seg_start = (tile_end - ntile) * MOE_TM
    n_used = tile_end[-1]
    tiles = jnp.arange(MOE_TILES, dtype=jnp.int32)
    live = jnp.minimum(tiles, n_used - 1)
    expert = jnp.sum((live[:, None] >= tile_end[None, :]).astype(jnp.int32), axis=1)
    first = ((live == (tile_end - ntile)[expert]) & (tiles < n_used)).astype(jnp.int32)
    pad_lo = jnp.concatenate([seg_start + cnt, (n_used * MOE_TM).reshape(1)])
    pad_hi = jnp.concatenate([tile_end * MOE_TM, jnp.full((1,), MOE_TILES * MOE_TM, jnp.int32)])
    return seg_start, expert, first, n_used.reshape(1), pad_lo, pad_hi


SUBLANES = 8


def _row_copy(src, i, dst, tile, sub, sem):
    return pltpu.make_async_copy(src.at[pl.ds(i, 1), :], dst.at[tile, pl.ds(sub, 1), :], sem)


def _gather_rows(src, index_of, dst, n_rows, sem):
    def body(tile, carry):
        for sub in range(SUBLANES):
            _row_copy(src, index_of(tile * SUBLANES + sub), dst, tile, sub, sem).start()
        return carry
    lax.fori_loop(0, n_rows // SUBLANES, body, 0)


def _wait_rows(src, dst, n_rows, sem):
    for _ in range(n_rows):
        _row_copy(src, 0, dst, 0, 0, sem).wait()


def _invert_kernel(pos0_ref, pos1_ref, pad_lo_ref, pad_hi_ref, src_ref):
    def fill(s, carry):
        src_ref[s] = 0
        return carry

    def put(t, carry):
        src_ref[pos0_ref[t]] = t
        src_ref[pos1_ref[t]] = t
        return carry

    for k in range(N_EXPERTS + 1):
        lax.fori_loop(pad_lo_ref[k], pad_hi_ref[k], fill, 0)
    lax.fori_loop(0, T_ALL, put, 0, unroll=16)


def _invert(pos, pad_lo, pad_hi):
    return pl.pallas_call(
        _invert_kernel,
        grid_spec=pltpu.PrefetchScalarGridSpec(
            num_scalar_prefetch=4,
            grid=(1,),
            in_specs=[],
            out_specs=pl.BlockSpec(memory_space=pltpu.SMEM),
        ),
        out_shape=jax.ShapeDtypeStruct((MOE_TILES * MOE_TM,), jnp.int32),
        compiler_params=_params(("arbitrary",)),
        name="invert",
    )(pos[0], pos[1], pad_lo, pad_hi)


def _moe_kernel(src_ref, expert_ref, first_ref, nused_ref, h_ref, wg_ref, wu_ref, wd_ref, y_ref,
                xa, xb, sems, wg_bf, wu_bf, wd_bf):
    i = pl.program_id(0)
    n_used = nused_ref[0]

    def gather(tile, buf, sem, unrolled):
        base = jnp.minimum(tile, MOE_TILES - 1) * MOE_TM
        if unrolled:
            for r in range(MOE_TM):
                _row_copy(h_ref, src_ref[base + r], buf, r // SUBLANES, r % SUBLANES, sem).start(priority=r % 2)
        else:
            _gather_rows(h_ref, lambda r: src_ref[base + r], buf, MOE_TM, sem)

    @pl.when(i == 0)
    def _():
        gather(0, xa, sems.at[0], False)
        gather(1, xb, sems.at[1], False)

    @pl.when((first_ref[jnp.minimum(i, MOE_TILES - 1)] == 1) & (i < MOE_TILES))
    def _():
        wg_bf[...] = wg_ref[0, 0].astype(BF16)
        wu_bf[...] = wu_ref[0, 0].astype(BF16)
        wd_bf[...] = wd_ref[0, 0].astype(BF16)

    for parity, buf in enumerate((xa, xb)):
        sem = sems.at[parity]
        mine = i % 2 == parity

        @pl.when((i < n_used) & mine)
        def _():
            _wait_rows(h_ref, buf, MOE_TM, sem)
            x = buf[...].reshape(MOE_TM, D_MODEL).astype(BF16)
            gather(i + 2, buf, sem, True)
            act = jax.nn.silu(_dot(x, wg_bf[...])) * _dot(x, wu_bf[...])
            y_ref[...] = _dot(act.astype(BF16), wd_bf[...])

        @pl.when((i >= n_used) & (i < n_used + 2) & mine)
        def _():
            _wait_rows(h_ref, buf, MOE_TM, sem)

    @pl.when((i >= n_used) & (i < MOE_TILES))
    def _():
        y_ref[...] = jnp.zeros_like(y_ref)


def _moe(h2, src, expert, first, n_used, w_gate, w_up, w_down, layer):
    tile = lambda i: jnp.minimum(i, MOE_TILES - 1)
    wspec = lambda shape: pl.BlockSpec((1, 1) + shape, lambda i, sr, ex, fi, nu: (layer, ex[tile(i)], 0, 0))
    xbuf = pltpu.VMEM((MOE_TM // SUBLANES, SUBLANES, D_MODEL), F32)
    return pl.pallas_call(
        _moe_kernel,
        grid_spec=pltpu.PrefetchScalarGridSpec(
            num_scalar_prefetch=4,
            grid=(MOE_TILES + 2,),
            in_specs=[pl.BlockSpec(memory_space=pl.ANY),
                      wspec((D_MODEL, EXPERT_FF)), wspec((D_MODEL, EXPERT_FF)), wspec((EXPERT_FF, D_MODEL))],
            out_specs=pl.BlockSpec((MOE_TM, D_MODEL), lambda i, sr, ex, fi, nu: (tile(i), 0)),
            scratch_shapes=[xbuf, xbuf, pltpu.SemaphoreType.DMA((2,)),
                            pltpu.VMEM((D_MODEL, EXPERT_FF), BF16), pltpu.VMEM((D_MODEL, EXPERT_FF), BF16),
                            pltpu.VMEM((EXPERT_FF, D_MODEL), BF16)],
        ),
        out_shape=jax.ShapeDtypeStruct((MOE_TILES * MOE_TM, D_MODEL), F32),
        compiler_params=_params(("arbitrary",), disable_bounds_checks=True),
        name=f"moe{layer}",
    )(src, expert, first, n_used, h2, w_gate, w_up, w_down)


def _combine_kernel(last, pos0_ref, pos1_ref, ys_ref, x_ref, w_ref, mod_ref, g_ref, nmod_ref, *refs):
    outs, (ya, yb, sems) = refs[:2], refs[2:]
    i = pl.program_id(0)
    tm = x_ref.shape[0]
    n = T_ALL // tm

    def gather(tile, buf, sem, unrolled):
        base = jnp.minimum(tile, n - 1) * tm
        for choice, pos_ref in enumerate((pos0_ref, pos1_ref)):
            if unrolled:
                for r in range(tm):
                    _row_copy(ys_ref, pos_ref[base + r], buf.at[choice], r // SUBLANES, r % SUBLANES,
                              sem).start(priority=r % 2)
            else:
                _gather_rows(ys_ref, lambda r: pos_ref[base + r], buf.at[choice], tm, sem)

    @pl.when(i == 0)
    def _():
        gather(0, ya, sems.at[0], False)
        gather(1, yb, sems.at[1], False)

    for parity, buf in enumerate((ya, yb)):
        sem = sems.at[parity]
        mine = i % 2 == parity

        @pl.when((i < n) & mine)
        def _():
            _wait_rows(ys_ref, buf.at[0], TOP_K * tm, sem)
            w0 = _row_to_col(w_ref[0:1, :])
            w1 = _row_to_col(w_ref[1:2, :])
            y = w0 * buf[0].reshape(tm, D_MODEL) + w1 * buf[1].reshape(tm, D_MODEL)
            gather(i + 2, buf, sem, True)
            x = x_ref[...] + mod_ref[0, 0, G2:G2 + 1, :] * y
            if last:
                yc_ref, ys_out_ref = outs
                final = _rms(x) * g_ref[0]

                @pl.when(i < T_CTX // tm)
                def _():
                    yc_ref[...] = final

                @pl.when(i >= T_CTX // tm)
                def _():
                    ys_out_ref[...] = final
            else:
                xo_ref, h_ref = outs
                xo_ref[...] = x
                h_ref[...] = _mod_norm(x, g_ref[0], nmod_ref, SH1, SC1).astype(BF16)

        @pl.when((i >= n) & mine)
        def _():
            _wait_rows(ys_ref, buf.at[0], TOP_K * tm, sem)


def _combine(ys, pos, w_t, x, mods, norm_g, layer):
    tm = COMBINE_TM
    n = T_ALL // tm
    last = layer == DEPTH - 1
    nxt = 0 if last else layer + 1
    tile = lambda i: jnp.minimum(i, n - 1)
    tok = pl.BlockSpec((tm, D_MODEL), lambda i, p0, p1: (tile(i), 0))
    if last:
        out_specs = _two_group_specs(tm, D_MODEL)
        out_shape = [jax.ShapeDtypeStruct((T_CTX, D_MODEL), F32), jax.ShapeDtypeStruct((T_SMP, D_MODEL), F32)]
    else:
        out_specs = [tok, tok]
        out_shape = [jax.ShapeDtypeStruct((T_ALL, D_MODEL), F32), jax.ShapeDtypeStruct((T_ALL, D_MODEL), BF16)]
    ybuf = pltpu.VMEM((TOP_K, tm // SUBLANES, SUBLANES, D_MODEL), F32)
    return pl.pallas_call(
        functools.partial(_combine_kernel, last),
        grid_spec=pltpu.PrefetchScalarGridSpec(
            num_scalar_prefetch=2,
            grid=(n + 2,),
            in_specs=[pl.BlockSpec(memory_space=pl.ANY),
                      tok,
                      pl.BlockSpec((TOP_K, tm), lambda i, p0, p1: (0, tile(i))),
                      _mod_spec(layer, tm),
                      pl.BlockSpec((1, 1, D_MODEL), lambda i, p0, p1: (nxt, 0, 0)),
                      _mod_spec(nxt, tm)],
            out_specs=out_specs,
            scratch_shapes=[ybuf, ybuf, pltpu.SemaphoreType.DMA((2,))],
        ),
        out_shape=out_shape,
        compiler_params=_params(("arbitrary",), disable_bounds_checks=True),
        name=f"combine{layer}",
    )(pos[0], pos[1], ys, x, w_t, mods, norm_g, mods)


def _stack_kernel(*refs):
    ins, outs = refs[:2 * DEPTH], refs[2 * DEPTH:]
    for t, out_ref in enumerate(outs):
        for l in range(DEPTH):
            out_ref[0, l] = ins[t * DEPTH + l][...]


def _stack_kv(ks, vs):
    seq = pl.BlockSpec((SEQ, NA_WIDTH), lambda b: (b, 0))
    out = pl.BlockSpec((1, DEPTH, SEQ, NA_WIDTH), lambda b: (b, 0, 0, 0))
    shape = jax.ShapeDtypeStruct((BATCH, DEPTH, SEQ, NA_WIDTH), F32)
    return pl.pallas_call(
        _stack_kernel,
        grid=(BATCH,),
        in_specs=[seq] * (2 * DEPTH),
        out_specs=[out, out],
        out_shape=[shape, shape],
        compiler_params=_params(("arbitrary",)),
        name="stack_kv",
    )(*ks, *vs)


def kernel(x_prompt, x_sample, c, cache_k, cache_v, c_ctx, ada_w, ada_b, norm1_g, w_in, rpb, pool_w, pool_scale,
           w_branch_a, w_branch_p, w_out, norm2_g, router_w, router_bias, moe_w_gate, moe_w_up, moe_w_down, final_g):
    x_ctx = x_prompt.reshape(T_CTX, D_MODEL)
    x_smp = x_sample.reshape(T_SMP, D_MODEL)
    cond = jnp.concatenate([c_ctx[None, :], c, jnp.zeros((N_COND - 1 - DEC_BATCH, D_MODEL), F32)], axis=0)
    mods = _ada(cond, ada_w, ada_b).reshape(DEPTH, N_COND, N_MOD, D_MODEL)
    rpb_pad = jnp.pad(rpb.astype(F32), ((0, 0), (0, 0), (0, 1), (0, LANES - rpb.shape[-1])))
    w_ba, w_bp, w_o = w_branch_a.astype(BF16), w_branch_p.astype(BF16), w_out.astype(BF16)
    router_wt = router_w.T
    norm1 = norm1_g.reshape(DEPTH, 1, D_MODEL)

    h = _pre(x_ctx, x_smp, norm1_g, mods)
    x_parts = (x_ctx, x_smp)
    new_ks, new_vs = [], []
    for l in range(DEPTH):
        q, kc, ks, vc, vs, u, gates = _inproj(h, w_in, l)
        new_ks.append(kc)
        new_vs.append(vc)
        oa_c, op_c = _ctx_mixer(q, kc, vc, u, pool_w, pool_scale, l)
        oa_s, op_s = _na_mixer(q, ks, vs, u, cache_k, cache_v, rpb_pad, pool_w, pool_scale, l)
        mix = _branch(oa_c, op_c, oa_s, op_s, gates, w_ba, w_bp, l)
        x, h2, logits_t = _merge(mix, x_parts, mods, w_o, norm2_g, router_wt, l)
        idx_t, w_t, rank_t, counts = _route(logits_t, router_bias)
        seg_start, expert, first, n_used, pad_lo, pad_hi = _tile_plan(counts[:, 0])
        pos = _slots(idx_t, rank_t, seg_start)
        src = _invert(pos, pad_lo, pad_hi)
        ys = _moe(h2, src, expert, first, n_used, moe_w_gate, moe_w_up, moe_w_down, l)
        if l < DEPTH - 1:
            x, h = _combine(ys, pos, w_t, x, mods, norm1, l)
            x_parts = (x,)
        else:
            y_ctx, y_smp = _combine(ys, pos, w_t, x, mods, final_g.reshape(1, 1, D_MODEL), l)
    new_k, new_v = _stack_kv(new_ks, new_vs)
    heads = (BATCH, DEPTH, SEQ, NA_HEADS, NA_HEAD_DIM)
    return (y_ctx.reshape(BATCH, SEQ, D_MODEL), y_smp.reshape(DEC_BATCH, DEC_SEQ, D_MODEL),
            new_k.reshape(heads), new_v.reshape(heads))
```

```python
import functools

import numpy as np
import jax
import jax.numpy as jnp
from jax import lax
from jax.experimental import pallas as pl
from jax.experimental.pallas import tpu as pltpu

D_MODEL = 2048
BATCH = 32
SEQ = 256
DEPTH = 2
DEC_BATCH = 2
DEC_SEQ = 2048
PAST_LEN = 256
GRID_W = 64
GRID_ROWS = DEC_SEQ // GRID_W
NA_HEADS = 16
NA_HEAD_DIM = 64
NA_WIDTH = NA_HEADS * NA_HEAD_DIM
WIN_H = 8
WIN_W = 16
POOL_WIDTH = D_MODEL // 2
POOL_WINDOWS = (2, 4, 8, 16)
POOL_GROUPS = len(POOL_WINDOWS)
POOL_GC = POOL_WIDTH // POOL_GROUPS
IN_WIDTH = 3 * NA_WIDTH + POOL_WIDTH + 2 * D_MODEL
N_EXPERTS = 16
N_GROUPS = 4
EXPERTS_PER_GROUP = N_EXPERTS // N_GROUPS
TOP_K = 2
EXPERT_FF = D_MODEL // 4
N_MOD = 6
EPS = 1e-6

T_CTX = BATCH * SEQ
T_SMP = DEC_BATCH * DEC_SEQ
T_ALL = T_CTX + T_SMP
N_COND = 8
NA_KEYS = WIN_H * GRID_W
MASKED = -1e30

F32 = jnp.float32
BF16 = jnp.bfloat16
VMEM_LIMIT = 56 * 1024 * 1024

SH1, SC1, G1, SH2, SC2, G2 = range(6)
LANES = 128
PAIR = 2 * NA_HEAD_DIM
ROUTE_BLOCK = 512
MOE_TM = 256
MOE_TILES = T_ALL * TOP_K // MOE_TM + N_EXPERTS
COMBINE_TM = 256


def _params(sem, vmem=VMEM_LIMIT, **kw):
    return pltpu.CompilerParams(dimension_semantics=sem, vmem_limit_bytes=vmem, **kw)


def _cond_of_tile(i, tm):
    t0 = i * tm
    return jnp.where(t0 < T_CTX, 0, 1 + (t0 - T_CTX) // DEC_SEQ)


def _mod_spec(layer, tm):
    last = T_ALL // tm - 1
    return pl.BlockSpec((1, 1, N_MOD, D_MODEL),
                        lambda i, *_: (layer, _cond_of_tile(jnp.minimum(i, last), tm), 0, 0))


def _two_group_specs(tm, width):
    nc = T_CTX // tm
    ns = T_SMP // tm
    return [pl.BlockSpec((tm, width), lambda i, *_: (jnp.minimum(i, nc - 1), 0)),
            pl.BlockSpec((tm, width), lambda i, *_: (jnp.clip(i - nc, 0, ns - 1), 0))]


def _rms(x):
    return x * lax.rsqrt(jnp.mean(x * x, axis=-1, keepdims=True) + EPS)


def _mod_norm(x, g, mod_ref, shift_row, scale_row):
    return (_rms(x) * g * (1.0 + mod_ref[0, 0, scale_row:scale_row + 1, :])
            + mod_ref[0, 0, shift_row:shift_row + 1, :])


def _dot(a, b):
    return jnp.dot(a, b, preferred_element_type=F32)


def _dot_nt(a, b):
    return lax.dot_general(a, b, (((1,), (1,)), ((), ())), preferred_element_type=F32)


def _split2(x):
    x1 = x.astype(BF16)
    return x1, (x - x1.astype(F32)).astype(BF16)


def _row_to_col(v):
    n = v.shape[1]
    r = lax.broadcasted_iota(jnp.int32, (n, n), 0)
    c = lax.broadcasted_iota(jnp.int32, (n, n), 1)
    return jnp.sum(jnp.where(r == c, v, 0.0), axis=1, keepdims=True)


def _ada_kernel(c_ref, w_ref, b_ref, o_ref):
    s = jax.nn.silu(c_ref[...]).astype(BF16)
    o_ref[0] = _dot(s, w_ref[0].astype(BF16)) + b_ref[0]


def _ada(cond, ada_w, ada_b):
    tn = 1024
    nj = N_MOD * D_MODEL // tn
    return pl.pallas_call(
        _ada_kernel,
        grid=(DEPTH, nj),
        in_specs=[
            pl.BlockSpec((N_COND, D_MODEL), lambda l, j: (0, 0)),
            pl.BlockSpec((1, D_MODEL, tn), lambda l, j: (l, 0, j)),
            pl.BlockSpec((1, 1, tn), lambda l, j: (l, 0, j)),
        ],
        out_specs=pl.BlockSpec((1, N_COND, tn), lambda l, j: (l, 0, j)),
        out_shape=jax.ShapeDtypeStruct((DEPTH, N_COND, N_MOD * D_MODEL), F32),
        compiler_params=_params(("arbitrary", "arbitrary")),
        name="ada",
    )(cond, ada_w, ada_b.reshape(DEPTH, 1, N_MOD * D_MODEL))


def _pre_kernel(xc_ref, xs_ref, g_ref, mod_ref, h_ref):
    is_ctx = pl.program_id(0) < T_CTX // h_ref.shape[0]
    x = jnp.where(is_ctx, xc_ref[...], xs_ref[...])
    h_ref[...] = _mod_norm(x, g_ref[0], mod_ref, SH1, SC1).astype(BF16)


def _pre(x_ctx, x_smp, norm_g, mods):
    tm = 512
    return pl.pallas_call(
        _pre_kernel,
        grid=(T_ALL // tm,),
        in_specs=_two_group_specs(tm, D_MODEL) + [
            pl.BlockSpec((1, 1, D_MODEL), lambda i: (0, 0, 0)),
            _mod_spec(0, tm),
        ],
        out_specs=pl.BlockSpec((tm, D_MODEL), lambda i: (i, 0)),
        out_shape=jax.ShapeDtypeStruct((T_ALL, D_MODEL), BF16),
        compiler_params=_params(("arbitrary",)),
        name="pre",
    )(x_ctx, x_smp, norm_g.reshape(DEPTH, 1, D_MODEL), mods)


INPROJ_TM = 512
INPROJ_TN = 1024
_NI = T_ALL // INPROJ_TM
_NC = T_CTX // INPROJ_TM
_COLS = {name: (start // INPROJ_TN, width // INPROJ_TN) for name, start, width in (
    ("q", 0, NA_WIDTH), ("k", NA_WIDTH, NA_WIDTH), ("v", 2 * NA_WIDTH, NA_WIDTH),
    ("u", 3 * NA_WIDTH, POOL_WIDTH), ("g", 3 * NA_WIDTH + POOL_WIDTH, 2 * D_MODEL))}


def _visit(name, rows, last_row):
    j0, nj = _COLS[name]

    def index(j, i):
        row = jnp.where(j < j0, 0, jnp.where(j < j0 + nj, rows(i), last_row))
        return row, jnp.clip(j - j0, 0, nj - 1)
    return index


def _inproj_kernel(h_ref, w_ref, q_ref, kc_ref, ks_ref, vc_ref, vs_ref, u_ref, g_ref, wbf_ref):
    j = pl.program_id(0)
    i = pl.program_id(1)

    @pl.when(i == 0)
    def _():
        wbf_ref[...] = w_ref[0].astype(BF16)

    def project():
        return _dot(h_ref[...], wbf_ref[...])

    def during(name):
        j0, nj = _COLS[name]
        return (j >= j0) & (j < j0 + nj)

    @pl.when(during("q"))
    def _():
        q_ref[...] = (project() * NA_HEAD_DIM ** -0.5).astype(BF16)

    for name, ctx_ref, smp_ref in (("k", kc_ref, ks_ref), ("v", vc_ref, vs_ref)):
        @pl.when(during(name) & (i < _NC))
        def _():
            ctx_ref[...] = project()

        @pl.when(during(name) & (i >= _NC))
        def _():
            smp_ref[...] = project().astype(BF16)

    @pl.when(during("u"))
    def _():
        u_ref[...] = project()

    @pl.when(during("g"))
    def _():
        g_ref[...] = project().astype(BF16)


def _inproj(h, w_in, layer):
    tm, tn = INPROJ_TM, INPROJ_TN
    blk = pl.BlockSpec
    every = lambda i: i
    ctx_rows = lambda i: jnp.minimum(i, _NC - 1)
    smp_rows = lambda i: jnp.clip(i - _NC, 0, _NI - _NC - 1)
    sds = jax.ShapeDtypeStruct
    return pl.pallas_call(
        _inproj_kernel,
        grid=(IN_WIDTH // tn, _NI),
        in_specs=[blk((tm, D_MODEL), lambda j, i: (i, 0)),
                  blk((1, D_MODEL, tn), lambda j, i: (layer, 0, j))],
        out_specs=[blk((tm, tn), _visit("q", every, _NI - 1)),
                   blk((tm, tn), _visit("k", ctx_rows, _NC - 1)),
                   blk((tm, tn), _visit("k", smp_rows, _NI - _NC - 1)),
                   blk((tm, tn), _visit("v", ctx_rows, _NC - 1)),
                   blk((tm, tn), _visit("v", smp_rows, _NI - _NC - 1)),
                   blk((tm, tn), _visit("u", every, _NI - 1)),
                   blk((tm, tn), _visit("g", every, _NI - 1))],
        out_shape=[sds((T_ALL, NA_WIDTH), BF16),
                   sds((T_CTX, NA_WIDTH), F32), sds((T_SMP, NA_WIDTH), BF16),
                   sds((T_CTX, NA_WIDTH), F32), sds((T_SMP, NA_WIDTH), BF16),
                   sds((T_ALL, POOL_WIDTH), F32),
                   sds((T_ALL, 2 * D_MODEL), BF16)],
        scratch_shapes=[pltpu.VMEM((D_MODEL, tn), BF16)],
        compiler_params=_params(("arbitrary", "arbitrary")),
        name=f"inproj{layer}",
    )(h, w_in)


def _pool_consts(n, seg):
    t = np.arange(n)
    pos = t % seg
    base = t - pos
    mats = np.zeros((POOL_GROUPS, n, n), np.float32)
    cnts = np.zeros((POOL_GROUPS, n, 1), np.float32)
    for gi, w in enumerate(POOL_WINDOWS):
        lo = np.clip(pos - w // 2, 0, seg)
        hi = np.clip(pos + w - w // 2, 0, seg)
        s = t[None, :]
        mats[gi] = ((s >= (base + lo)[:, None]) & (s < (base + hi)[:, None])).astype(np.float32)
        cnts[gi, :, 0] = hi - lo
    return jnp.asarray(mats, BF16), jnp.asarray(cnts, F32)


def _pool(u_ref, band_ref, cnt_ref, pw_ref, ps_ref, op_ref):
    for gi in range(POOL_GROUPS):
        cols = slice(gi * POOL_GC, (gi + 1) * POOL_GC)
        ug = u_ref[:, cols]
        u1 = ug.astype(BF16)
        u2 = (ug - u1.astype(F32)).astype(BF16)
        u3 = (ug - u1.astype(F32) - u2.astype(F32)).astype(BF16)
        band = band_ref[gi]
        wsum = _dot(band, u1) + _dot(band, u2) + _dot(band, u3)
        d = (wsum / cnt_ref[gi] - ug).astype(BF16)
        y = _dot(d, pw_ref[0, gi].astype(BF16))
        op_ref[:, cols] = (y * ps_ref[0, :, cols]).astype(BF16)


def _softmax_av(s_parts, v_parts):
    m = s_parts[0].max(axis=-1, keepdims=True)
    for s in s_parts[1:]:
        m = jnp.maximum(m, s.max(axis=-1, keepdims=True))
    den = None
    acc = None
    for s, v in zip(s_parts, v_parts):
        p = jnp.exp(s - m)
        ps = p.sum(axis=-1, keepdims=True)
        den = ps if den is None else den + ps
        pv = _dot(p.astype(BF16), v)
        acc = pv if acc is None else acc + pv
    return acc / den


def _attend_pair(q, ks, vs, biases):
    m = q.shape[0]
    first = lax.broadcasted_iota(jnp.int32, (m, PAIR), 1) < NA_HEAD_DIM
    qf = q.astype(F32)
    q2 = jnp.concatenate([jnp.where(first, qf, 0.0), jnp.where(first, 0.0, qf)], axis=0).astype(BF16)
    scores = [_dot_nt(q2, k) if b is None else _dot_nt(q2, k) + b for k, b in zip(ks, biases)]
    o2 = _softmax_av(scores, vs)
    return jnp.where(first, o2[:m], o2[m:])


def _ctx_mixer_kernel(q_ref, k_ref, v_ref, u_ref, band_ref, cnt_ref, pw_ref, ps_ref, oa_ref, op_ref):
    for hp in range(NA_HEADS // 2):
        cols = slice(hp * PAIR, (hp + 1) * PAIR)
        o = _attend_pair(q_ref[:, cols], [k_ref[:, cols].astype(BF16)], [v_ref[:, cols].astype(BF16)], [None])
        oa_ref[:, cols] = o.astype(BF16)
    _pool(u_ref, band_ref, cnt_ref, pw_ref, ps_ref, op_ref)


def _ctx_mixer(q, kc, vc, u, pool_w, pool_scale, layer):
    band, cnt = _pool_consts(SEQ, SEQ)
    seq = lambda width: pl.BlockSpec((SEQ, width), lambda s: (s, 0))
    return pl.pallas_call(
        _ctx_mixer_kernel,
        grid=(BATCH,),
        in_specs=[
            seq(NA_WIDTH), seq(NA_WIDTH), seq(NA_WIDTH), seq(POOL_WIDTH),
            pl.BlockSpec((POOL_GROUPS, SEQ, SEQ), lambda s: (0, 0, 0)),
            pl.BlockSpec((POOL_GROUPS, SEQ, 1), lambda s: (0, 0, 0)),
            pl.BlockSpec((1, POOL_GROUPS, POOL_GC, POOL_GC), lambda s: (layer, 0, 0, 0)),
            pl.BlockSpec((1, 1, POOL_WIDTH), lambda s: (layer, 0, 0)),
        ],
        out_specs=[seq(NA_WIDTH), seq(POOL_WIDTH)],
        out_shape=[jax.ShapeDtypeStruct((T_CTX, NA_WIDTH), BF16),
                   jax.ShapeDtypeStruct((T_CTX, POOL_WIDTH), BF16)],
        compiler_params=_params(("arbitrary",)),
        name=f"ctx_mixer{layer}",
    )(q, kc, vc, u, band, cnt, pool_w, pool_scale.reshape(DEPTH, 1, POOL_WIDTH))


def _na_build_bias(rpb_ref, bias_ref, r, rs):
    lanes = 2 * GRID_W
    qc = lax.broadcasted_iota(jnp.int32, (GRID_W, lanes), 0)
    lane = lax.broadcasted_iota(jnp.int32, (GRID_W, lanes), 1)
    kc = jnp.bitwise_and(lane, GRID_W - 1)
    cstart = jnp.clip(qc - WIN_W // 2, 0, GRID_W - WIN_W)
    valid = (kc >= cstart) & (kc < cstart + WIN_W)
    for h in range(NA_HEADS):
        for jp in range(WIN_H // 2):
            halves = []
            for j, shift in ((2 * jp, lanes - (WIN_W - 1)), (2 * jp + 1, GRID_W - (WIN_W - 1))):
                dr = rs + j - r + (WIN_H - 1)
                row = jnp.broadcast_to(rpb_ref[0, h, pl.ds(dr, 1), :], (GRID_W, lanes))
                halves.append(pltpu.roll(row, shift, 1, stride=1, stride_axis=0))
            tile = jnp.where(lane < GRID_W, halves[0], halves[1])
            bias_ref[h, :, jp * lanes:(jp + 1) * lanes] = jnp.where(valid, tile, MASKED)


def _na_mixer_kernel(q_ref, k_ref, v_ref, ck_ref, cv_ref, rpb_ref, u_ref, band_ref, cnt_ref, pw_ref, ps_ref,
                     oa_ref, op_ref, bias_ref, ckb_ref, cvb_ref):
    r = pl.program_id(1)
    rs = jnp.clip(r - WIN_H // 2, 0, GRID_ROWS - WIN_H)

    @pl.when(r == 0)
    def _():
        ckb_ref[...] = ck_ref[0, 0].astype(BF16)
        cvb_ref[...] = cv_ref[0, 0].astype(BF16)

    @pl.when((r <= WIN_H // 2) | (r > GRID_ROWS - WIN_H // 2))
    def _():
        _na_build_bias(rpb_ref, bias_ref, r, rs)

    keys = pl.ds(pl.multiple_of(rs * GRID_W, GRID_W), NA_KEYS)
    for hp in range(NA_HEADS // 2):
        cols = slice(hp * PAIR, (hp + 1) * PAIR)
        bias = bias_ref[2 * hp:2 * hp + 2].reshape(2 * GRID_W, NA_KEYS)
        o = _attend_pair(q_ref[:, cols], [k_ref[keys, cols], ckb_ref[:, cols]],
                         [v_ref[keys, cols], cvb_ref[:, cols]], [bias, None])
        oa_ref[:, cols] = o.astype(BF16)
    _pool(u_ref, band_ref, cnt_ref, pw_ref, ps_ref, op_ref)


def _na_mixer(q, ks, vs, u, cache_k, cache_v, rpb_pad, pool_w, pool_scale, layer):
    band, cnt = _pool_consts(GRID_W, GRID_W)
    row0 = T_CTX // GRID_W
    row = lambda width: pl.BlockSpec((GRID_W, width), lambda b, r: (row0 + b * GRID_ROWS + r, 0))
    batch = pl.BlockSpec((DEC_SEQ, NA_WIDTH), lambda b, r: (b, 0))
    cblk = pl.BlockSpec((1, 1, PAST_LEN, NA_WIDTH), lambda b, r: (b, layer, 0, 0))
    out = lambda width: pl.BlockSpec((GRID_W, width), lambda b, r: (b * GRID_ROWS + r, 0))
    return pl.pallas_call(
        _na_mixer_kernel,
        grid=(DEC_BATCH, GRID_ROWS),
        in_specs=[
            row(NA_WIDTH), batch, batch, cblk, cblk,
            pl.BlockSpec((1,) + rpb_pad.shape[1:], lambda b, r: (layer, 0, 0, 0)),
            row(POOL_WIDTH),
            pl.BlockSpec((POOL_GROUPS, GRID_W, GRID_W), lambda b, r: (0, 0, 0)),
            pl.BlockSpec((POOL_GROUPS, GRID_W, 1), lambda b, r: (0, 0, 0)),
            pl.BlockSpec((1, POOL_GROUPS, POOL_GC, POOL_GC), lambda b, r: (layer, 0, 0, 0)),
            pl.BlockSpec((1, 1, POOL_WIDTH), lambda b, r: (layer, 0, 0)),
        ],
        out_specs=[out(NA_WIDTH), out(POOL_WIDTH)],
        out_shape=[jax.ShapeDtypeStruct((T_SMP, NA_WIDTH), BF16),
                   jax.ShapeDtypeStruct((T_SMP, POOL_WIDTH), BF16)],
        scratch_shapes=[pltpu.VMEM((NA_HEADS, GRID_W, NA_KEYS), F32),
                        pltpu.VMEM((PAST_LEN, NA_WIDTH), BF16), pltpu.VMEM((PAST_LEN, NA_WIDTH), BF16)],
        compiler_params=_params(("arbitrary", "arbitrary")),
        name=f"na_mixer{layer}",
    )(q, ks, vs,
      cache_k.reshape(DEC_BATCH, DEPTH, PAST_LEN, NA_WIDTH), cache_v.reshape(DEC_BATCH, DEPTH, PAST_LEN, NA_WIDTH),
      rpb_pad, u, band, cnt, pool_w, pool_scale.reshape(DEPTH, 1, POOL_WIDTH))


MERGE_TM = 512
MERGE_HALF = MERGE_TM // 2


def _halves():
    return [slice(k * MERGE_HALF, (k + 1) * MERGE_HALF) for k in range(2)]


def _branch_kernel(oac_ref, opc_ref, oas_ref, ops_ref, ga_ref, gp_ref, wba_ref, wbp_ref, mix_ref):
    is_ctx = pl.program_id(0) < T_CTX // MERGE_TM
    for rows in _halves():
        a = _dot(jnp.where(is_ctx, oac_ref[rows, :], oas_ref[rows, :]), wba_ref[0])
        p = _dot(jnp.where(is_ctx, opc_ref[rows, :], ops_ref[rows, :]), wbp_ref[0])
        mix = (jax.nn.sigmoid(ga_ref[rows, :].astype(F32)) * a
               + jax.nn.sigmoid(gp_ref[rows, :].astype(F32)) * p)
        mix_ref[rows, :] = mix.astype(BF16)


def _branch(oa_c, op_c, oa_s, op_s, gates, w_ba, w_bp, layer):
    tm = MERGE_TM
    full = lambda shape: pl.BlockSpec(shape, lambda i: (layer,) + (0,) * (len(shape) - 1))
    na, pool = _two_group_specs(tm, NA_WIDTH), _two_group_specs(tm, POOL_WIDTH)
    return pl.pallas_call(
        _branch_kernel,
        grid=(T_ALL // tm,),
        in_specs=[na[0], pool[0], na[1], pool[1],
                  pl.BlockSpec((tm, D_MODEL), lambda i: (i, 0)),
                  pl.BlockSpec((tm, D_MODEL), lambda i: (i, 1)),
                  full((1, NA_WIDTH, D_MODEL)),
                  full((1, POOL_WIDTH, D_MODEL))],
        out_specs=pl.BlockSpec((tm, D_MODEL), lambda i: (i, 0)),
        out_shape=jax.ShapeDtypeStruct((T_ALL, D_MODEL), BF16),
        compiler_params=_params(("arbitrary",)),
        name=f"branch{layer}",
    )(oa_c, op_c, oa_s, op_s, gates, gates, w_ba, w_bp)


def _merge_kernel(n_x, mix_ref, *refs):
    x_refs, (mod_ref, wout_ref, n2g_ref, rwt_ref, xo_ref, h2_ref, lg_ref) = refs[:n_x], refs[n_x:]
    is_ctx = pl.program_id(0) < T_CTX // MERGE_TM
    w1, w2 = _split2(rwt_ref[...])
    for rows in _halves():
        m = _dot(mix_ref[rows, :], wout_ref[0])
        x_in = x_refs[0][rows, :] if n_x == 1 else jnp.where(is_ctx, x_refs[0][rows, :], x_refs[1][rows, :])
        x = x_in + mod_ref[0, 0, G1:G1 + 1, :] * m
        xo_ref[rows, :] = x
        h2 = _mod_norm(x, n2g_ref[0], mod_ref, SH2, SC2)
        h2_ref[rows, :] = h2
        t1, t2 = _split2(h2)
        lg_ref[:, rows] = _dot_nt(w1, t1) + _dot_nt(w1, t2) + _dot_nt(w2, t1)


def _merge(mix, x_parts, mods, w_out, norm2_g, router_wt, layer):
    tm = MERGE_TM
    full = lambda shape: pl.BlockSpec(shape, lambda i: (layer,) + (0,) * (len(shape) - 1))
    tok = pl.BlockSpec((tm, D_MODEL), lambda i: (i, 0))
    x_specs = [tok] if len(x_parts) == 1 else _two_group_specs(tm, D_MODEL)
    return pl.pallas_call(
        functools.partial(_merge_kernel, len(x_parts)),
        grid=(T_ALL // tm,),
        in_specs=[tok] + x_specs + [
            _mod_spec(layer, tm),
            full((1, D_MODEL, D_MODEL)),
            full((1, 1, D_MODEL)),
            pl.BlockSpec((N_EXPERTS, D_MODEL), lambda i: (0, 0)),
        ],
        out_specs=[tok, tok, pl.BlockSpec((N_EXPERTS, tm), lambda i: (0, i))],
        out_shape=[jax.ShapeDtypeStruct((T_ALL, D_MODEL), F32),
                   jax.ShapeDtypeStruct((T_ALL, D_MODEL), F32),
                   jax.ShapeDtypeStruct((N_EXPERTS, T_ALL), F32)],
        compiler_params=_params(("arbitrary",)),
        name=f"merge{layer}",
    )(mix, *x_parts, mods, w_out, norm2_g.reshape(DEPTH, 1, D_MODEL), router_wt)


def _route_kernel(lg_ref, rb_ref, tri_ref, idx_ref, w_ref, rank_ref, cnt_ref, carry_ref):
    @pl.when(pl.program_id(0) == 0)
    def _():
        carry_ref[...] = jnp.zeros_like(carry_ref)

    scores = jax.nn.sigmoid(lg_ref[...])
    sel = scores + rb_ref[...]
    gscore = []
    for g in range(N_GROUPS):
        a, b, c, d = [sel[EXPERTS_PER_GROUP * g + i:EXPERTS_PER_GROUP * g + i + 1, :] for i in range(4)]
        hi1, lo1, hi2, lo2 = jnp.maximum(a, b), jnp.minimum(a, b), jnp.maximum(c, d), jnp.minimum(c, d)
        gscore.append(jnp.maximum(hi1, hi2) + jnp.maximum(jnp.minimum(hi1, hi2), jnp.maximum(lo1, lo2)))
    best = jnp.zeros_like(gscore[0], dtype=jnp.int32)
    bestv = gscore[0]
    for g in range(1, N_GROUPS):
        upd = gscore[g] > bestv
        best = jnp.where(upd, g, best)
        bestv = jnp.where(upd, gscore[g], bestv)
    row = lax.broadcasted_iota(jnp.int32, sel.shape, 0)
    masked = jnp.where(row // EXPERTS_PER_GROUP == best, sel, -jnp.inf)
    i0 = jnp.min(jnp.where(masked == masked.max(axis=0, keepdims=True), row, N_EXPERTS), axis=0, keepdims=True)
    masked = jnp.where(row == i0, -jnp.inf, masked)
    i1 = jnp.min(jnp.where(masked == masked.max(axis=0, keepdims=True), row, N_EXPERTS), axis=0, keepdims=True)
    s0 = jnp.sum(jnp.where(row == i0, scores, 0.0), axis=0, keepdims=True)
    s1 = jnp.sum(jnp.where(row == i1, scores, 0.0), axis=0, keepdims=True)
    den = s0 + s1
    idx_ref[...] = jnp.concatenate([i0, i1], axis=0)
    w_ref[...] = jnp.concatenate([s0 / den, s1 / den], axis=0)

    hit = jnp.where((row == i0) | (row == i1), 1.0, 0.0)
    before = _dot(hit.astype(BF16), tri_ref[...]) + carry_ref[:, 0:1]
    r0 = jnp.sum(jnp.where(row == i0, before, 0.0), axis=0, keepdims=True)
    r1 = jnp.sum(jnp.where(row == i1, before, 0.0), axis=0, keepdims=True)
    rank_ref[...] = jnp.concatenate([r0, r1], axis=0).astype(jnp.int32)
    carry_ref[...] = carry_ref[...] + jnp.sum(hit, axis=1, keepdims=True)
    cnt_ref[...] = carry_ref[...]


def _route(logits_t, router_bias):
    tn = ROUTE_BLOCK
    tri = jnp.asarray(np.triu(np.ones((tn, tn), np.float32), k=1), BF16)
    pair = lambda dt: jax.ShapeDtypeStruct((TOP_K, T_ALL), dt)
    return pl.pallas_call(
        _route_kernel,
        grid=(T_ALL // tn,),
        in_specs=[pl.BlockSpec((N_EXPERTS, tn), lambda i: (0, i)),
                  pl.BlockSpec((N_EXPERTS, 1), lambda i: (0, 0)),
                  pl.BlockSpec((tn, tn), lambda i: (0, 0))],
        out_specs=[pl.BlockSpec((TOP_K, tn), lambda i: (0, i)),
                   pl.BlockSpec((TOP_K, tn), lambda i: (0, i)),
                   pl.BlockSpec((TOP_K, tn), lambda i: (0, i)),
                   pl.BlockSpec((N_EXPERTS, LANES), lambda i: (0, 0))],
        out_shape=[pair(jnp.int32), pair(F32), pair(jnp.int32),
                   jax.ShapeDtypeStruct((N_EXPERTS, LANES), F32)],
        scratch_shapes=[pltpu.VMEM((N_EXPERTS, LANES), F32)],
        compiler_params=_params(("arbitrary",)),
        name="route",
    )(logits_t, router_bias.reshape(N_EXPERTS, 1).astype(F32), tri)


def _slot_kernel(idx_ref, rank_ref, off_ref, pos_ref):
    row = lax.broadcasted_iota(jnp.int32, (N_EXPERTS, idx_ref.shape[1]), 0)
    for j in range(TOP_K):
        off = jnp.sum(jnp.where(row == idx_ref[j:j + 1, :], off_ref[...], 0), axis=0, keepdims=True)
        pos_ref[j:j + 1, :] = off + rank_ref[j:j + 1, :]


def _slots(idx_t, rank_t, seg_start):
    tn = ROUTE_BLOCK
    blk = pl.BlockSpec((TOP_K, tn), lambda i: (0, i))
    return pl.pallas_call(
        _slot_kernel,
        grid=(T_ALL // tn,),
        in_specs=[blk, blk, pl.BlockSpec((N_EXPERTS, 1), lambda i: (0, 0))],
        out_specs=blk,
        out_shape=jax.ShapeDtypeStruct((TOP_K, T_ALL), jnp.int32),
        compiler_params=_params(("arbitrary",)),
        name="slots",
    )(idx_t, rank_t, seg_start.reshape(N_EXPERTS, 1))


def _tile_plan(counts):
    cnt = counts.astype(jnp.int32)
    ntile = (cnt + MOE_TM - 1) // MOE_TM
    tile_end = jnp.cumsum(ntile)
    seg_start = (tile_end - ntile) * MOE_TM
    n_used = tile_end[-1]
    tiles = jnp.arange(MOE_TILES, dtype=jnp.int32)
    live = jnp.minimum(tiles, n_used - 1)
    expert = jnp.sum((live[:, None] >= tile_end[None, :]).astype(jnp.int32), axis=1)
    first = ((live == (tile_end - ntile)[expert]) & (tiles < n_used)).astype(jnp.int32)
    pad_lo = jnp.concatenate([seg_start + cnt, (n_used * MOE_TM).reshape(1)])
    pad_hi = jnp.concatenate([tile_end * MOE_TM, jnp.full((1,), MOE_TILES * MOE_TM, jnp.int32)])
    return seg_start, expert, first, n_used.reshape(1), pad_lo, pad_hi


SUBLANES = 8


def _row_copy(src, i, dst, tile, sub, sem):
    return pltpu.make_async_copy(src.at[pl.ds(i, 1), :], dst.at[tile, pl.ds(sub, 1), :], sem)


def _gather_rows(src, index_of, dst, n_rows, sem):
    def body(tile, carry):
        for sub in range(SUBLANES):
            _row_copy(src, index_of(tile * SUBLANES + sub), dst, tile, sub, sem).start()
        return carry
    lax.fori_loop(0, n_rows // SUBLANES, body, 0)


def _wait_rows(src, dst, n_rows, sem):
    for _ in range(n_rows):
        _row_copy(src, 0, dst, 0, 0, sem).wait()


def _invert_kernel(pos0_ref, pos1_ref, pad_lo_ref, pad_hi_ref, src_ref):
    def fill(s, carry):
        src_ref[s] = 0
        return carry

    def put(t, carry):
        src_ref[pos0_ref[t]] = t
        src_ref[pos1_ref[t]] = t
        return carry

    for k in range(N_EXPERTS + 1):
        lax.fori_loop(pad_lo_ref[k], pad_hi_ref[k], fill, 0)
    lax.fori_loop(0, T_ALL, put, 0, unroll=16)


def _invert(pos, pad_lo, pad_hi):
    return pl.pallas_call(
        _invert_kernel,
        grid_spec=pltpu.PrefetchScalarGridSpec(
            num_scalar_prefetch=4,
            grid=(1,),
            in_specs=[],
            out_specs=pl.BlockSpec(memory_space=pltpu.SMEM),
        ),
        out_shape=jax.ShapeDtypeStruct((MOE_TILES * MOE_TM,), jnp.int32),
        compiler_params=_params(("arbitrary",)),
        name="invert",
    )(pos[0], pos[1], pad_lo, pad_hi)


def _dispatch_kernel(src_ref, nused_ref, h_ref, xs_ref, xa, xb, sems):
    i = pl.program_id(0)
    n_used = nused_ref[0]

    def gather(tile, buf, sem, unrolled):
        base = jnp.minimum(tile, MOE_TILES - 1) * MOE_TM
        if unrolled:
            for r in range(MOE_TM):
                _row_copy(h_ref, src_ref[base + r], buf, r // SUBLANES, r % SUBLANES, sem).start(priority=r % 2)
        else:
            _gather_rows(h_ref, lambda r: src_ref[base + r], buf, MOE_TM, sem)

    @pl.when(i == 0)
    def _():
        gather(0, xa, sems.at[0], False)
        gather(1, xb, sems.at[1], False)

    for parity, buf in enumerate((xa, xb)):
        sem = sems.at[parity]
        mine = i % 2 == parity

        @pl.when((i < n_used) & mine)
        def _():
            _wait_rows(h_ref, buf, MOE_TM, sem)
            x = buf[...].reshape(MOE_TM, D_MODEL)
            gather(i + 2, buf, sem, True)
            xs_ref[...] = x.astype(BF16)

        @pl.when((i >= n_used) & (i < n_used + 2) & mine)
        def _():
            _wait_rows(h_ref, buf, MOE_TM, sem)

    @pl.when((i >= n_used) & (i < MOE_TILES))
    def _():
        xs_ref[...] = jnp.zeros_like(xs_ref)


def _dispatch(h2, src, n_used):
    tile = lambda i: jnp.minimum(i, MOE_TILES - 1)
    xbuf = pltpu.VMEM((MOE_TM // SUBLANES, SUBLANES, D_MODEL), F32)
    return pl.pallas_call(
        _dispatch_kernel,
        grid_spec=pltpu.PrefetchScalarGridSpec(
            num_scalar_prefetch=2,
            grid=(MOE_TILES + 2,),
            in_specs=[pl.BlockSpec(memory_space=pl.ANY)],
            out_specs=pl.BlockSpec((MOE_TM, D_MODEL), lambda i, sr, nu: (tile(i), 0)),
            scratch_shapes=[xbuf, xbuf, pltpu.SemaphoreType.DMA((2,))],
        ),
        out_shape=jax.ShapeDtypeStruct((MOE_TILES * MOE_TM, D_MODEL), BF16),
        compiler_params=_params(("arbitrary",), disable_bounds_checks=True),
        name="dispatch",
    )(src, n_used, h2)


def _moe_kernel(expert_ref, first_ref, nused_ref, x_ref, wg_ref, wu_ref, wd_ref, y_ref, wg_bf, wu_bf, wd_bf):
    i = pl.program_id(0)

    @pl.when(first_ref[i] == 1)
    def _():
        wg_bf[...] = wg_ref[0, 0].astype(BF16)
        wu_bf[...] = wu_ref[0, 0].astype(BF16)
        wd_bf[...] = wd_ref[0, 0].astype(BF16)

    @pl.when(i < nused_ref[0])
    def _():
        x = x_ref[...]
        act = jax.nn.silu(_dot(x, wg_bf[...])) * _dot(x, wu_bf[...])
        y_ref[...] = _dot(act.astype(BF16), wd_bf[...])

    @pl.when(i >= nused_ref[0])
    def _():
        y_ref[...] = jnp.zeros_like(y_ref)


def _moe(xs, expert, first, n_used, w_gate, w_up, w_down, layer):
    wspec = lambda shape: pl.BlockSpec((1, 1) + shape, lambda i, ex, fi, nu: (layer, ex[i], 0, 0))
    rows = pl.BlockSpec((MOE_TM, D_MODEL), lambda i, ex, fi, nu: (i, 0))
    return pl.pallas_call(
        _moe_kernel,
        grid_spec=pltpu.PrefetchScalarGridSpec(
            num_scalar_prefetch=3,
            grid=(MOE_TILES,),
            in_specs=[rows, wspec((D_MODEL, EXPERT_FF)), wspec((D_MODEL, EXPERT_FF)), wspec((EXPERT_FF, D_MODEL))],
            out_specs=rows,
            scratch_shapes=[pltpu.VMEM((D_MODEL, EXPERT_FF), BF16), pltpu.VMEM((D_MODEL, EXPERT_FF), BF16),
                            pltpu.VMEM((EXPERT_FF, D_MODEL), BF16)],
        ),
        out_shape=jax.ShapeDtypeStruct((MOE_TILES * MOE_TM, D_MODEL), F32),
        compiler_params=_params(("arbitrary",)),
        name=f"moe{layer}",
    )(expert, first, n_used, xs, w_gate, w_up, w_down)


def _combine_kernel(last, pos0_ref, pos1_ref, ys_ref, x_ref, w_ref, mod_ref, g_ref, nmod_ref, *refs):
    outs, (ya, yb, sems) = refs[:2], refs[2:]
    i = pl.program_id(0)
    tm = x_ref.shape[0]
    n = T_ALL // tm

    def gather(tile, buf, sem, unrolled):
        base = jnp.minimum(tile, n - 1) * tm
        for choice, pos_ref in enumerate((pos0_ref, pos1_ref)):
            if unrolled:
                for r in range(tm):
                    _row_copy(ys_ref, pos_ref[base + r], buf.at[choice], r // SUBLANES, r % SUBLANES,
                              sem).start(priority=r % 2)
            else:
                _gather_rows(ys_ref, lambda r: pos_ref[base + r], buf.at[choice], tm, sem)

    @pl.when(i == 0)
    def _():
        gather(0, ya, sems.at[0], False)
        gather(1, yb, sems.at[1], False)

    for parity, buf in enumerate((ya, yb)):
        sem = sems.at[parity]
        mine = i % 2 == parity

        @pl.when((i < n) & mine)
        def _():
            _wait_rows(ys_ref, buf.at[0], TOP_K * tm, sem)
            w0 = _row_to_col(w_ref[0:1, :])
            w1 = _row_to_col(w_ref[1:2, :])
            y = w0 * buf[0].reshape(tm, D_MODEL) + w1 * buf[1].reshape(tm, D_MODEL)
            gather(i + 2, buf, sem, True)
            x = x_ref[...] + mod_ref[0, 0, G2:G2 + 1, :] * y
            if last:
                yc_ref, ys_out_ref = outs
                final = _rms(x) * g_ref[0]

                @pl.when(i < T_CTX // tm)
                def _():
                    yc_ref[...] = final

                @pl.when(i >= T_CTX // tm)
                def _():
                    ys_out_ref[...] = final
            else:
                xo_ref, h_ref = outs
                xo_ref[...] = x
                h_ref[...] = _mod_norm(x, g_ref[0], nmod_ref, SH1, SC1).astype(BF16)

        @pl.when((i >= n) & mine)
        def _():
            _wait_rows(ys_ref, buf.at[0], TOP_K * tm, sem)


def _combine(ys, pos, w_t, x, mods, norm_g, layer):
    tm = COMBINE_TM
    n = T_ALL // tm
    last = layer == DEPTH - 1
    nxt = 0 if last else layer + 1
    tile = lambda i: jnp.minimum(i, n - 1)
    tok = pl.BlockSpec((tm, D_MODEL), lambda i, p0, p1: (tile(i), 0))
    if last:
        out_specs = _two_group_specs(tm, D_MODEL)
        out_shape = [jax.ShapeDtypeStruct((T_CTX, D_MODEL), F32), jax.ShapeDtypeStruct((T_SMP, D_MODEL), F32)]
    else:
        out_specs = [tok, tok]
        out_shape = [jax.ShapeDtypeStruct((T_ALL, D_MODEL), F32), jax.ShapeDtypeStruct((T_ALL, D_MODEL), BF16)]
    ybuf = pltpu.VMEM((TOP_K, tm // SUBLANES, SUBLANES, D_MODEL), F32)
    return pl.pallas_call(
        functools.partial(_combine_kernel, last),
        grid_spec=pltpu.PrefetchScalarGridSpec(
            num_scalar_prefetch=2,
            grid=(n + 2,),
            in_specs=[pl.BlockSpec(memory_space=pl.ANY),
                      tok,
                      pl.BlockSpec((TOP_K, tm), lambda i, p0, p1: (0, tile(i))),
                      _mod_spec(layer, tm),
                      pl.BlockSpec((1, 1, D_MODEL), lambda i, p0, p1: (nxt, 0, 0)),
                      _mod_spec(nxt, tm)],
            out_specs=out_specs,
            scratch_shapes=[ybuf, ybuf, pltpu.SemaphoreType.DMA((2,))],
        ),
        out_shape=out_shape,
        compiler_params=_params(("arbitrary",), disable_bounds_checks=True),
        name=f"combine{layer}",
    )(pos[0], pos[1], ys, x, w_t, mods, norm_g, mods)


def _stack_kernel(*refs):
    ins, outs = refs[:2 * DEPTH], refs[2 * DEPTH:]
    for t, out_ref in enumerate(outs):
        for l in range(DEPTH):
            out_ref[0, l] = ins[t * DEPTH + l][...]


def _stack_kv(ks, vs):
    seq = pl.BlockSpec((SEQ, NA_WIDTH), lambda b: (b, 0))
    out = pl.BlockSpec((1, DEPTH, SEQ, NA_WIDTH), lambda b: (b, 0, 0, 0))
    shape = jax.ShapeDtypeStruct((BATCH, DEPTH, SEQ, NA_WIDTH), F32)
    return pl.pallas_call(
        _stack_kernel,
        grid=(BATCH,),
        in_specs=[seq] * (2 * DEPTH),
        out_specs=[out, out],
        out_shape=[shape, shape],
        compiler_params=_params(("arbitrary",)),
        name="stack_kv",
    )(*ks, *vs)


def kernel(x_prompt, x_sample, c, cache_k, cache_v, c_ctx, ada_w, ada_b, norm1_g, w_in, rpb, pool_w, pool_scale,
           w_branch_a, w_branch_p, w_out, norm2_g, router_w, router_bias, moe_w_gate, moe_w_up, moe_w_down, final_g):
    x_ctx = x_prompt.reshape(T_CTX, D_MODEL)
    x_smp = x_sample.reshape(T_SMP, D_MODEL)
    cond = jnp.concatenate([c_ctx[None, :], c, jnp.zeros((N_COND - 1 - DEC_BATCH, D_MODEL), F32)], axis=0)
    mods = _ada(cond, ada_w, ada_b).reshape(DEPTH, N_COND, N_MOD, D_MODEL)
    rpb_pad = jnp.pad(rpb.astype(F32), ((0, 0), (0, 0), (0, 1), (0, LANES - rpb.shape[-1])))
    w_ba, w_bp, w_o = w_branch_a.astype(BF16), w_branch_p.astype(BF16), w_out.astype(BF16)
    router_wt = router_w.T
    norm1 = norm1_g.reshape(DEPTH, 1, D_MODEL)

    h = _pre(x_ctx, x_smp, norm1_g, mods)
    x_parts = (x_ctx, x_smp)
    new_ks, new_vs = [], []
    for l in range(DEPTH):
        q, kc, ks, vc, vs, u, gates = _inproj(h, w_in, l)
        new_ks.append(kc)
        new_vs.append(vc)
        oa_c, op_c = _ctx_mixer(q, kc, vc, u, pool_w, pool_scale, l)
        oa_s, op_s = _na_mixer(q, ks, vs, u, cache_k, cache_v, rpb_pad, pool_w, pool_scale, l)
        mix = _branch(oa_c, op_c, oa_s, op_s, gates, w_ba, w_bp, l)
        x, h2, logits_t = _merge(mix, x_parts, mods, w_o, norm2_g, router_wt, l)
        idx_t, w_t, rank_t, counts = _route(logits_t, router_bias)
        seg_start, expert, first, n_used, pad_lo, pad_hi = _tile_plan(counts[:, 0])
        pos = _slots(idx_t, rank_t, seg_start)
        src = _invert(pos, pad_lo, pad_hi)
        xs = _dispatch(h2, src, n_used)
        ys = _moe(xs, expert, first, n_used, moe_w_gate, moe_w_up, moe_w_down, l)
        if l < DEPTH - 1:
            x, h = _combine(ys, pos, w_t, x, mods, norm1, l)
            x_parts = (x,)
        else:
            y_ctx, y_smp = _combine(ys, pos, w_t, x, mods, final_g.reshape(1, 1, D_MODEL), l)
    new_k, new_v = _stack_kv(new_ks, new_vs)
    heads = (BATCH, DEPTH, SEQ, NA_HEADS, NA_HEAD_DIM)
    return (y_ctx.reshape(BATCH, SEQ, D_MODEL), y_smp.reshape(DEC_BATCH, DEC_SEQ, D_MODEL),
            new_k.reshape(heads), new_v.reshape(heads))
```

```python
import functools

import numpy as np
import jax
import jax.numpy as jnp
from jax import lax
from jax.experimental import pallas as pl
from jax.experimental.pallas import tpu as pltpu

D_MODEL = 2048
BATCH = 32
SEQ = 256
DEPTH = 2
DEC_BATCH = 2
DEC_SEQ = 2048
PAST_LEN = 256
GRID_W = 64
GRID_ROWS = DEC_SEQ // GRID_W
NA_HEADS = 16
NA_HEAD_DIM = 64
NA_WIDTH = NA_HEADS * NA_HEAD_DIM
WIN_H = 8
WIN_W = 16
POOL_WIDTH = D_MODEL // 2
POOL_WINDOWS = (2, 4, 8, 16)
POOL_GROUPS = len(POOL_WINDOWS)
POOL_GC = POOL_WIDTH // POOL_GROUPS
IN_WIDTH = 3 * NA_WIDTH + POOL_WIDTH + 2 * D_MODEL
N_EXPERTS = 16
N_GROUPS = 4
EXPERTS_PER_GROUP = N_EXPERTS // N_GROUPS
TOP_K = 2
EXPERT_FF = D_MODEL // 4
N_MOD = 6
EPS = 1e-6

T_CTX = BATCH * SEQ
T_SMP = DEC_BATCH * DEC_SEQ
T_ALL = T_CTX + T_SMP
N_COND = 8
NA_KEYS = WIN_H * GRID_W
MASKED = -1e30

F32 = jnp.float32
BF16 = jnp.bfloat16
VMEM_LIMIT = 56 * 1024 * 1024

SH1, SC1, G1, SH2, SC2, G2 = range(6)
LANES = 128
PAIR = 2 * NA_HEAD_DIM
ROUTE_BLOCK = 512
MOE_TM = 256
MOE_TILES = T_ALL * TOP_K // MOE_TM + N_EXPERTS
COMBINE_TM = 256
GATHER_STRIDE = 16


def _params(sem, vmem=VMEM_LIMIT, **kw):
    return pltpu.CompilerParams(dimension_semantics=sem, vmem_limit_bytes=vmem, **kw)


def _cond_of_tile(i, tm):
    t0 = i * tm
    return jnp.where(t0 < T_CTX, 0, 1 + (t0 - T_CTX) // DEC_SEQ)


def _mod_spec(layer, tm):
    last = T_ALL // tm - 1
    return pl.BlockSpec((1, 1, N_MOD, D_MODEL),
                        lambda i, *_: (layer, _cond_of_tile(jnp.minimum(i, last), tm), 0, 0))


def _two_group_specs(tm, width):
    nc = T_CTX // tm
    ns = T_SMP // tm
    return [pl.BlockSpec((tm, width), lambda i, *_: (jnp.minimum(i, nc - 1), 0)),
            pl.BlockSpec((tm, width), lambda i, *_: (jnp.clip(i - nc, 0, ns - 1), 0))]


def _rms(x):
    return x * lax.rsqrt(jnp.mean(x * x, axis=-1, keepdims=True) + EPS)


def _mod_norm(x, g, mod_ref, shift_row, scale_row):
    return (_rms(x) * g * (1.0 + mod_ref[0, 0, scale_row:scale_row + 1, :])
            + mod_ref[0, 0, shift_row:shift_row + 1, :])


def _dot(a, b):
    return jnp.dot(a, b, preferred_element_type=F32)


def _dot_nt(a, b):
    return lax.dot_general(a, b, (((1,), (1,)), ((), ())), preferred_element_type=F32)


def _split2(x):
    x1 = x.astype(BF16)
    return x1, (x - x1.astype(F32)).astype(BF16)


def _row_to_col(v):
    n = v.shape[1]
    r = lax.broadcasted_iota(jnp.int32, (n, n), 0)
    c = lax.broadcasted_iota(jnp.int32, (n, n), 1)
    return jnp.sum(jnp.where(r == c, v, 0.0), axis=1, keepdims=True)


def _ada_kernel(c_ref, w_ref, b_ref, o_ref):
    s = jax.nn.silu(c_ref[...]).astype(BF16)
    o_ref[0] = _dot(s, w_ref[0].astype(BF16)) + b_ref[0]


def _ada(cond, ada_w, ada_b):
    tn = 1024
    nj = N_MOD * D_MODEL // tn
    return pl.pallas_call(
        _ada_kernel,
        grid=(DEPTH, nj),
        in_specs=[
            pl.BlockSpec((N_COND, D_MODEL), lambda l, j: (0, 0)),
            pl.BlockSpec((1, D_MODEL, tn), lambda l, j: (l, 0, j)),
            pl.BlockSpec((1, 1, tn), lambda l, j: (l, 0, j)),
        ],
        out_specs=pl.BlockSpec((1, N_COND, tn), lambda l, j: (l, 0, j)),
        out_shape=jax.ShapeDtypeStruct((DEPTH, N_COND, N_MOD * D_MODEL), F32),
        compiler_params=_params(("arbitrary", "arbitrary")),
        name="ada",
    )(cond, ada_w, ada_b.reshape(DEPTH, 1, N_MOD * D_MODEL))


def _pre_kernel(xc_ref, xs_ref, g_ref, mod_ref, h_ref):
    is_ctx = pl.program_id(0) < T_CTX // h_ref.shape[0]
    x = jnp.where(is_ctx, xc_ref[...], xs_ref[...])
    h_ref[...] = _mod_norm(x, g_ref[0], mod_ref, SH1, SC1).astype(BF16)


def _pre(x_ctx, x_smp, norm_g, mods):
    tm = 512
    return pl.pallas_call(
        _pre_kernel,
        grid=(T_ALL // tm,),
        in_specs=_two_group_specs(tm, D_MODEL) + [
            pl.BlockSpec((1, 1, D_MODEL), lambda i: (0, 0, 0)),
            _mod_spec(0, tm),
        ],
        out_specs=pl.BlockSpec((tm, D_MODEL), lambda i: (i, 0)),
        out_shape=jax.ShapeDtypeStruct((T_ALL, D_MODEL), BF16),
        compiler_params=_params(("arbitrary",)),
        name="pre",
    )(x_ctx, x_smp, norm_g.reshape(DEPTH, 1, D_MODEL), mods)


INPROJ_TM = 512
INPROJ_TN = 1024
_NI = T_ALL // INPROJ_TM
_NC = T_CTX // INPROJ_TM
_COLS = {name: (start // INPROJ_TN, width // INPROJ_TN) for name, start, width in (
    ("q", 0, NA_WIDTH), ("k", NA_WIDTH, NA_WIDTH), ("v", 2 * NA_WIDTH, NA_WIDTH),
    ("u", 3 * NA_WIDTH, POOL_WIDTH), ("g", 3 * NA_WIDTH + POOL_WIDTH, 2 * D_MODEL))}


def _visit(name, rows, last_row):
    j0, nj = _COLS[name]

    def index(j, i):
        row = jnp.where(j < j0, 0, jnp.where(j < j0 + nj, rows(i), last_row))
        return row, jnp.clip(j - j0, 0, nj - 1)
    return index


def _inproj_kernel(h_ref, w_ref, q_ref, kc_ref, ks_ref, vc_ref, vs_ref, u_ref, g_ref, wbf_ref):
    j = pl.program_id(0)
    i = pl.program_id(1)

    @pl.when(i == 0)
    def _():
        wbf_ref[...] = w_ref[0].astype(BF16)

    def project():
        return _dot(h_ref[...], wbf_ref[...])

    def during(name):
        j0, nj = _COLS[name]
        return (j >= j0) & (j < j0 + nj)

    @pl.when(during("q"))
    def _():
        q_ref[...] = (project() * NA_HEAD_DIM ** -0.5).astype(BF16)

    for name, ctx_ref, smp_ref in (("k", kc_ref, ks_ref), ("v", vc_ref, vs_ref)):
        @pl.when(during(name) & (i < _NC))
        def _():
            ctx_ref[...] = project()

        @pl.when(during(name) & (i >= _NC))
        def _():
            smp_ref[...] = project().astype(BF16)

    @pl.when(during("u"))
    def _():
        u_ref[...] = project()

    @pl.when(during("g"))
    def _():
        g_ref[...] = project().astype(BF16)


def _inproj(h, w_in, layer):
    tm, tn = INPROJ_TM, INPROJ_TN
    blk = pl.BlockSpec
    every = lambda i: i
    ctx_rows = lambda i: jnp.minimum(i, _NC - 1)
    smp_rows = lambda i: jnp.clip(i - _NC, 0, _NI - _NC - 1)
    sds = jax.ShapeDtypeStruct
    return pl.pallas_call(
        _inproj_kernel,
        grid=(IN_WIDTH // tn, _NI),
        in_specs=[blk((tm, D_MODEL), lambda j, i: (i, 0)),
                  blk((1, D_MODEL, tn), lambda j, i: (layer, 0, j))],
        out_specs=[blk((tm, tn), _visit("q", every, _NI - 1)),
                   blk((tm, tn), _visit("k", ctx_rows, _NC - 1)),
                   blk((tm, tn), _visit("k", smp_rows, _NI - _NC - 1)),
                   blk((tm, tn), _visit("v", ctx_rows, _NC - 1)),
                   blk((tm, tn), _visit("v", smp_rows, _NI - _NC - 1)),
                   blk((tm, tn), _visit("u", every, _NI - 1)),
                   blk((tm, tn), _visit("g", every, _NI - 1))],
        out_shape=[sds((T_ALL, NA_WIDTH), BF16),
                   sds((T_CTX, NA_WIDTH), F32), sds((T_SMP, NA_WIDTH), BF16),
                   sds((T_CTX, NA_WIDTH), F32), sds((T_SMP, NA_WIDTH), BF16),
                   sds((T_ALL, POOL_WIDTH), F32),
                   sds((T_ALL, 2 * D_MODEL), BF16)],
        scratch_shapes=[pltpu.VMEM((D_MODEL, tn), BF16)],
        compiler_params=_params(("arbitrary", "arbitrary")),
        name=f"inproj{layer}",
    )(h, w_in)


def _pool_consts(n, seg):
    t = np.arange(n)
    pos = t % seg
    base = t - pos
    mats = np.zeros((POOL_GROUPS, n, n), np.float32)
    cnts = np.zeros((POOL_GROUPS, n, 1), np.float32)
    for gi, w in enumerate(POOL_WINDOWS):
        lo = np.clip(pos - w // 2, 0, seg)
        hi = np.clip(pos + w - w // 2, 0, seg)
        s = t[None, :]
        mats[gi] = ((s >= (base + lo)[:, None]) & (s < (base + hi)[:, None])).astype(np.float32)
        cnts[gi, :, 0] = hi - lo
    return jnp.asarray(mats, BF16), jnp.asarray(cnts, F32)


def _pool(u_ref, band_ref, cnt_ref, pw_ref, ps_ref, op_ref):
    for gi in range(POOL_GROUPS):
        cols = slice(gi * POOL_GC, (gi + 1) * POOL_GC)
        ug = u_ref[:, cols]
        u1 = ug.astype(BF16)
        u2 = (ug - u1.astype(F32)).astype(BF16)
        u3 = (ug - u1.astype(F32) - u2.astype(F32)).astype(BF16)
        band = band_ref[gi]
        wsum = _dot(band, u1) + _dot(band, u2) + _dot(band, u3)
        d = (wsum / cnt_ref[gi] - ug).astype(BF16)
        y = _dot(d, pw_ref[0, gi].astype(BF16))
        op_ref[:, cols] = (y * ps_ref[0, :, cols]).astype(BF16)


def _softmax_av(s_parts, v_parts):
    m = s_parts[0].max(axis=-1, keepdims=True)
    for s in s_parts[1:]:
        m = jnp.maximum(m, s.max(axis=-1, keepdims=True))
    den = None
    acc = None
    for s, v in zip(s_parts, v_parts):
        p = jnp.exp(s - m)
        ps = p.sum(axis=-1, keepdims=True)
        den = ps if den is None else den + ps
        pv = _dot(p.astype(BF16), v)
        acc = pv if acc is None else acc + pv
    return acc / den


def _attend_pair(q, ks, vs, biases):
    m = q.shape[0]
    first = lax.broadcasted_iota(jnp.int32, (m, PAIR), 1) < NA_HEAD_DIM
    qf = q.astype(F32)
    q2 = jnp.concatenate([jnp.where(first, qf, 0.0), jnp.where(first, 0.0, qf)], axis=0).astype(BF16)
    scores = [_dot_nt(q2, k) if b is None else _dot_nt(q2, k) + b for k, b in zip(ks, biases)]
    o2 = _softmax_av(scores, vs)
    return jnp.where(first, o2[:m], o2[m:])


def _ctx_mixer_kernel(q_ref, k_ref, v_ref, u_ref, band_ref, cnt_ref, pw_ref, ps_ref, oa_ref, op_ref):
    for hp in range(NA_HEADS // 2):
        cols = slice(hp * PAIR, (hp + 1) * PAIR)
        o = _attend_pair(q_ref[:, cols], [k_ref[:, cols].astype(BF16)], [v_ref[:, cols].astype(BF16)], [None])
        oa_ref[:, cols] = o.astype(BF16)
    _pool(u_ref, band_ref, cnt_ref, pw_ref, ps_ref, op_ref)


def _ctx_mixer(q, kc, vc, u, pool_w, pool_scale, layer):
    band, cnt = _pool_consts(SEQ, SEQ)
    seq = lambda width: pl.BlockSpec((SEQ, width), lambda s: (s, 0))
    return pl.pallas_call(
        _ctx_mixer_kernel,
        grid=(BATCH,),
        in_specs=[
            seq(NA_WIDTH), seq(NA_WIDTH), seq(NA_WIDTH), seq(POOL_WIDTH),
            pl.BlockSpec((POOL_GROUPS, SEQ, SEQ), lambda s: (0, 0, 0)),
            pl.BlockSpec((POOL_GROUPS, SEQ, 1), lambda s: (0, 0, 0)),
            pl.BlockSpec((1, POOL_GROUPS, POOL_GC, POOL_GC), lambda s: (layer, 0, 0, 0)),
            pl.BlockSpec((1, 1, POOL_WIDTH), lambda s: (layer, 0, 0)),
        ],
        out_specs=[seq(NA_WIDTH), seq(POOL_WIDTH)],
        out_shape=[jax.ShapeDtypeStruct((T_CTX, NA_WIDTH), BF16),
                   jax.ShapeDtypeStruct((T_CTX, POOL_WIDTH), BF16)],
        compiler_params=_params(("arbitrary",)),
        name=f"ctx_mixer{layer}",
    )(q, kc, vc, u, band, cnt, pool_w, pool_scale.reshape(DEPTH, 1, POOL_WIDTH))


def _na_build_bias(rpb_ref, bias_ref, r, rs):
    lanes = 2 * GRID_W
    qc = lax.broadcasted_iota(jnp.int32, (GRID_W, lanes), 0)
    lane = lax.broadcasted_iota(jnp.int32, (GRID_W, lanes), 1)
    kc = jnp.bitwise_and(lane, GRID_W - 1)
    cstart = jnp.clip(qc - WIN_W // 2, 0, GRID_W - WIN_W)
    valid = (kc >= cstart) & (kc < cstart + WIN_W)
    for h in range(NA_HEADS):
        for jp in range(WIN_H // 2):
            halves = []
            for j, shift in ((2 * jp, lanes - (WIN_W - 1)), (2 * jp + 1, GRID_W - (WIN_W - 1))):
                dr = rs + j - r + (WIN_H - 1)
                row = jnp.broadcast_to(rpb_ref[0, h, pl.ds(dr, 1), :], (GRID_W, lanes))
                halves.append(pltpu.roll(row, shift, 1, stride=1, stride_axis=0))
            tile = jnp.where(lane < GRID_W, halves[0], halves[1])
            bias_ref[h, :, jp * lanes:(jp + 1) * lanes] = jnp.where(valid, tile, MASKED)


def _na_mixer_kernel(q_ref, k_ref, v_ref, ck_ref, cv_ref, rpb_ref, u_ref, band_ref, cnt_ref, pw_ref, ps_ref,
                     oa_ref, op_ref, bias_ref, ckb_ref, cvb_ref):
    r = pl.program_id(1)
    rs = jnp.clip(r - WIN_H // 2, 0, GRID_ROWS - WIN_H)

    @pl.when(r == 0)
    def _():
        ckb_ref[...] = ck_ref[0, 0].astype(BF16)
        cvb_ref[...] = cv_ref[0, 0].astype(BF16)

    @pl.when((r <= WIN_H // 2) | (r > GRID_ROWS - WIN_H // 2))
    def _():
        _na_build_bias(rpb_ref, bias_ref, r, rs)

    keys = pl.ds(pl.multiple_of(rs * GRID_W, GRID_W), NA_KEYS)
    for hp in range(NA_HEADS // 2):
        cols = slice(hp * PAIR, (hp + 1) * PAIR)
        bias = bias_ref[2 * hp:2 * hp + 2].reshape(2 * GRID_W, NA_KEYS)
        o = _attend_pair(q_ref[:, cols], [k_ref[keys, cols], ckb_ref[:, cols]],
                         [v_ref[keys, cols], cvb_ref[:, cols]], [bias, None])
        oa_ref[:, cols] = o.astype(BF16)
    _pool(u_ref, band_ref, cnt_ref, pw_ref, ps_ref, op_ref)


def _na_mixer(q, ks, vs, u, cache_k, cache_v, rpb_pad, pool_w, pool_scale, layer):
    band, cnt = _pool_consts(GRID_W, GRID_W)
    row0 = T_CTX // GRID_W
    row = lambda width: pl.BlockSpec((GRID_W, width), lambda b, r: (row0 + b * GRID_ROWS + r, 0))
    batch = pl.BlockSpec((DEC_SEQ, NA_WIDTH), lambda b, r: (b, 0))
    cblk = pl.BlockSpec((1, 1, PAST_LEN, NA_WIDTH), lambda b, r: (b, layer, 0, 0))
    out = lambda width: pl.BlockSpec((GRID_W, width), lambda b, r: (b * GRID_ROWS + r, 0))
    return pl.pallas_call(
        _na_mixer_kernel,
        grid=(DEC_BATCH, GRID_ROWS),
        in_specs=[
            row(NA_WIDTH), batch, batch, cblk, cblk,
            pl.BlockSpec((1,) + rpb_pad.shape[1:], lambda b, r: (layer, 0, 0, 0)),
            row(POOL_WIDTH),
            pl.BlockSpec((POOL_GROUPS, GRID_W, GRID_W), lambda b, r: (0, 0, 0)),
            pl.BlockSpec((POOL_GROUPS, GRID_W, 1), lambda b, r: (0, 0, 0)),
            pl.BlockSpec((1, POOL_GROUPS, POOL_GC, POOL_GC), lambda b, r: (layer, 0, 0, 0)),
            pl.BlockSpec((1, 1, POOL_WIDTH), lambda b, r: (layer, 0, 0)),
        ],
        out_specs=[out(NA_WIDTH), out(POOL_WIDTH)],
        out_shape=[jax.ShapeDtypeStruct((T_SMP, NA_WIDTH), BF16),
                   jax.ShapeDtypeStruct((T_SMP, POOL_WIDTH), BF16)],
        scratch_shapes=[pltpu.VMEM((NA_HEADS, GRID_W, NA_KEYS), F32),
                        pltpu.VMEM((PAST_LEN, NA_WIDTH), BF16), pltpu.VMEM((PAST_LEN, NA_WIDTH), BF16)],
        compiler_params=_params(("arbitrary", "arbitrary")),
        name=f"na_mixer{layer}",
    )(q, ks, vs,
      cache_k.reshape(DEC_BATCH, DEPTH, PAST_LEN, NA_WIDTH), cache_v.reshape(DEC_BATCH, DEPTH, PAST_LEN, NA_WIDTH),
      rpb_pad, u, band, cnt, pool_w, pool_scale.reshape(DEPTH, 1, POOL_WIDTH))


MERGE_TM = 512
MERGE_HALF = MERGE_TM // 2


def _halves():
    return [slice(k * MERGE_HALF, (k + 1) * MERGE_HALF) for k in range(2)]


def _branch_kernel(oac_ref, opc_ref, oas_ref, ops_ref, ga_ref, gp_ref, wba_ref, wbp_ref, mix_ref):
    is_ctx = pl.program_id(0) < T_CTX // MERGE_TM
    for rows in _halves():
        a = _dot(jnp.where(is_ctx, oac_ref[rows, :], oas_ref[rows, :]), wba_ref[0])
        p = _dot(jnp.where(is_ctx, opc_ref[rows, :], ops_ref[rows, :]), wbp_ref[0])
        mix = (jax.nn.sigmoid(ga_ref[rows, :].astype(F32)) * a
               + jax.nn.sigmoid(gp_ref[rows, :].astype(F32)) * p)
        mix_ref[rows, :] = mix.astype(BF16)


def _branch(oa_c, op_c, oa_s, op_s, gates, w_ba, w_bp, layer):
    tm = MERGE_TM
    full = lambda shape: pl.BlockSpec(shape, lambda i: (layer,) + (0,) * (len(shape) - 1))
    na, pool = _two_group_specs(tm, NA_WIDTH), _two_group_specs(tm, POOL_WIDTH)
    return pl.pallas_call(
        _branch_kernel,
        grid=(T_ALL // tm,),
        in_specs=[na[0], pool[0], na[1], pool[1],
                  pl.BlockSpec((tm, D_MODEL), lambda i: (i, 0)),
                  pl.BlockSpec((tm, D_MODEL), lambda i: (i, 1)),
                  full((1, NA_WIDTH, D_MODEL)),
                  full((1, POOL_WIDTH, D_MODEL))],
        out_specs=pl.BlockSpec((tm, D_MODEL), lambda i: (i, 0)),
        out_shape=jax.ShapeDtypeStruct((T_ALL, D_MODEL), BF16),
        compiler_params=_params(("arbitrary",)),
        name=f"branch{layer}",
    )(oa_c, op_c, oa_s, op_s, gates, gates, w_ba, w_bp)


def _merge_kernel(n_x, mix_ref, *refs):
    x_refs, (mod_ref, wout_ref, n2g_ref, rwt_ref, xo_ref, h2_ref, lg_ref) = refs[:n_x], refs[n_x:]
    is_ctx = pl.program_id(0) < T_CTX // MERGE_TM
    w1, w2 = _split2(rwt_ref[...])
    for rows in _halves():
        m = _dot(mix_ref[rows, :], wout_ref[0])
        x_in = x_refs[0][rows, :] if n_x == 1 else jnp.where(is_ctx, x_refs[0][rows, :], x_refs[1][rows, :])
        x = x_in + mod_ref[0, 0, G1:G1 + 1, :] * m
        xo_ref[rows, :] = x
        h2 = _mod_norm(x, n2g_ref[0], mod_ref, SH2, SC2)
        h2_ref[rows, :] = h2
        t1, t2 = _split2(h2)
        lg_ref[:, rows] = _dot_nt(w1, t1) + _dot_nt(w1, t2) + _dot_nt(w2, t1)


def _merge(mix, x_parts, mods, w_out, norm2_g, router_wt, layer):
    tm = MERGE_TM
    full = lambda shape: pl.BlockSpec(shape, lambda i: (layer,) + (0,) * (len(shape) - 1))
    tok = pl.BlockSpec((tm, D_MODEL), lambda i: (i, 0))
    x_specs = [tok] if len(x_parts) == 1 else _two_group_specs(tm, D_MODEL)
    return pl.pallas_call(
        functools.partial(_merge_kernel, len(x_parts)),
        grid=(T_ALL // tm,),
        in_specs=[tok] + x_specs + [
            _mod_spec(layer, tm),
            full((1, D_MODEL, D_MODEL)),
            full((1, 1, D_MODEL)),
            pl.BlockSpec((N_EXPERTS, D_MODEL), lambda i: (0, 0)),
        ],
        out_specs=[tok, tok, pl.BlockSpec((N_EXPERTS, tm), lambda i: (0, i))],
        out_shape=[jax.ShapeDtypeStruct((T_ALL, D_MODEL), F32),
                   jax.ShapeDtypeStruct((T_ALL, D_MODEL), F32),
                   jax.ShapeDtypeStruct((N_EXPERTS, T_ALL), F32)],
        compiler_params=_params(("arbitrary",)),
        name=f"merge{layer}",
    )(mix, *x_parts, mods, w_out, norm2_g.reshape(DEPTH, 1, D_MODEL), router_wt)


def _route_kernel(lg_ref, rb_ref, tri_ref, idx_ref, w_ref, rank_ref, cnt_ref, carry_ref):
    @pl.when(pl.program_id(0) == 0)
    def _():
        carry_ref[...] = jnp.zeros_like(carry_ref)

    scores = jax.nn.sigmoid(lg_ref[...])
    sel = scores + rb_ref[...]
    gscore = []
    for g in range(N_GROUPS):
        a, b, c, d = [sel[EXPERTS_PER_GROUP * g + i:EXPERTS_PER_GROUP * g + i + 1, :] for i in range(4)]
        hi1, lo1, hi2, lo2 = jnp.maximum(a, b), jnp.minimum(a, b), jnp.maximum(c, d), jnp.minimum(c, d)
        gscore.append(jnp.maximum(hi1, hi2) + jnp.maximum(jnp.minimum(hi1, hi2), jnp.maximum(lo1, lo2)))
    best = jnp.zeros_like(gscore[0], dtype=jnp.int32)
    bestv = gscore[0]
    for g in range(1, N_GROUPS):
        upd = gscore[g] > bestv
        best = jnp.where(upd, g, best)
        bestv = jnp.where(upd, gscore[g], bestv)
    row = lax.broadcasted_iota(jnp.int32, sel.shape, 0)
    masked = jnp.where(row // EXPERTS_PER_GROUP == best, sel, -jnp.inf)
    i0 = jnp.min(jnp.where(masked == masked.max(axis=0, keepdims=True), row, N_EXPERTS), axis=0, keepdims=True)
    masked = jnp.where(row == i0, -jnp.inf, masked)
    i1 = jnp.min(jnp.where(masked == masked.max(axis=0, keepdims=True), row, N_EXPERTS), axis=0, keepdims=True)
    s0 = jnp.sum(jnp.where(row == i0, scores, 0.0), axis=0, keepdims=True)
    s1 = jnp.sum(jnp.where(row == i1, scores, 0.0), axis=0, keepdims=True)
    den = s0 + s1
    idx_ref[...] = jnp.concatenate([i0, i1], axis=0)
    w_ref[...] = jnp.concatenate([s0 / den, s1 / den], axis=0)

    hit = jnp.where((row == i0) | (row == i1), 1.0, 0.0)
    before = _dot(hit.astype(BF16), tri_ref[...]) + carry_ref[:, 0:1]
    r0 = jnp.sum(jnp.where(row == i0, before, 0.0), axis=0, keepdims=True)
    r1 = jnp.sum(jnp.where(row == i1, before, 0.0), axis=0, keepdims=True)
    rank_ref[...] = jnp.concatenate([r0, r1], axis=0).astype(jnp.int32)
    carry_ref[...] = carry_ref[...] + jnp.sum(hit, axis=1, keepdims=True)
    cnt_ref[...] = carry_ref[...]


def _route(logits_t, router_bias):
    tn = ROUTE_BLOCK
    tri = jnp.asarray(np.triu(np.ones((tn, tn), np.float32), k=1), BF16)
    pair = lambda dt: jax.ShapeDtypeStruct((TOP_K, T_ALL), dt)
    return pl.pallas_call(
        _route_kernel,
        grid=(T_ALL // tn,),
        in_specs=[pl.BlockSpec((N_EXPERTS, tn), lambda i: (0, i)),
                  pl.BlockSpec((N_EXPERTS, 1), lambda i: (0, 0)),
                  pl.BlockSpec((tn, tn), lambda i: (0, 0))],
        out_specs=[pl.BlockSpec((TOP_K, tn), lambda i: (0, i)),
                   pl.BlockSpec((TOP_K, tn), lambda i: (0, i)),
                   pl.BlockSpec((TOP_K, tn), lambda i: (0, i)),
                   pl.BlockSpec((N_EXPERTS, LANES), lambda i: (0, 0))],
        out_shape=[pair(jnp.int32), pair(F32), pair(jnp.int32),
                   jax.ShapeDtypeStruct((N_EXPERTS, LANES), F32)],
        scratch_shapes=[pltpu.VMEM((N_EXPERTS, LANES), F32)],
        compiler_params=_params(("arbitrary",)),
        name="route",
    )(logits_t, router_bias.reshape(N_EXPERTS, 1).astype(F32), tri)


def _slot_kernel(idx_ref, rank_ref, off_ref, pos_ref):
    row = lax.broadcasted_iota(jnp.int32, (N_EXPERTS, idx_ref.shape[1]), 0)
    for j in range(TOP_K):
        off = jnp.sum(jnp.where(row == idx_ref[j:j + 1, :], off_ref[...], 0), axis=0, keepdims=True)
        pos_ref[j:j + 1, :] = off + rank_ref[j:j + 1, :]


def _slots(idx_t, rank_t, seg_start):
    tn = ROUTE_BLOCK
    blk = pl.BlockSpec((TOP_K, tn), lambda i: (0, i))
    return pl.pallas_call(
        _slot_kernel,
        grid=(T_ALL // tn,),
        in_specs=[blk, blk, pl.BlockSpec((N_EXPERTS, 1), lambda i: (0, 0))],
        out_specs=blk,
        out_shape=jax.ShapeDtypeStruct((TOP_K, T_ALL), jnp.int32),
        compiler_params=_params(("arbitrary",)),
        name="slots",
    )(idx_t, rank_t, seg_start.reshape(N_EXPERTS, 1))


def _tile_plan(counts):
    cnt = counts.astype(jnp.int32)
    ntile = (cnt + MOE_TM - 1) // MOE_TM
    tile_end = jnp.cumsum(ntile)
    seg_start = (tile_end - ntile) * MOE_TM
    n_used = tile_end[-1]
    tiles = jnp.arange(MOE_TILES, dtype=jnp.int32)
    live = jnp.minimum(tiles, n_used - 1)
    expert = jnp.sum((live[:, None] >= tile_end[None, :]).astype(jnp.int32), axis=1)
    first = ((live == (tile_end - ntile)[expert]) & (tiles < n_used)).astype(jnp.int32)
    pad_lo = jnp.concatenate([seg_start + cnt, (n_used * MOE_TM).reshape(1)])
    pad_hi = jnp.concatenate([tile_end * MOE_TM, jnp.full((1,), MOE_TILES * MOE_TM, jnp.int32)])
    return seg_start, expert, first, n_used.reshape(1), pad_lo, pad_hi


SUBLANES = 8


def _row_copy(src, i, dst, tile, sub, sem):
    return pltpu.make_async_copy(src.at[pl.ds(i, 1), :], dst.at[tile, pl.ds(sub, 1), :], sem)


def _gather_rows(src, index_of, dst, n_rows, sem):
    def body(tile, carry):
        for sub in range(SUBLANES):
            _row_copy(src, index_of(tile * SUBLANES + sub), dst, tile, sub, sem).start()
        return carry
    lax.fori_loop(0, n_rows // SUBLANES, body, 0)


def _wait_rows(src, dst, n_rows, sem):
    for _ in range(n_rows):
        _row_copy(src, 0, dst, 0, 0, sem).wait()


def _invert_kernel(pos0_ref, pos1_ref, pad_lo_ref, pad_hi_ref, src_ref):
    def fill(s, carry):
        src_ref[s] = 0
        return carry

    def put(t, carry):
        src_ref[pos0_ref[t]] = t
        src_ref[pos1_ref[t]] = t
        return carry

    for k in range(N_EXPERTS + 1):
        lax.fori_loop(pad_lo_ref[k], pad_hi_ref[k], fill, 0)
    lax.fori_loop(0, T_ALL, put, 0, unroll=16)


def _invert(pos, pad_lo, pad_hi):
    return pl.pallas_call(
        _invert_kernel,
        grid_spec=pltpu.PrefetchScalarGridSpec(
            num_scalar_prefetch=4,
            grid=(1,),
            in_specs=[],
            out_specs=pl.BlockSpec(memory_space=pltpu.SMEM),
        ),
        out_shape=jax.ShapeDtypeStruct((MOE_TILES * MOE_TM,), jnp.int32),
        compiler_params=_params(("arbitrary",)),
        name="invert",
    )(pos[0], pos[1], pad_lo, pad_hi)


def _moe_kernel(src_ref, expert_ref, first_ref, nused_ref, h_ref, wg_ref, wu_ref, wd_ref, y_ref,
                xa, xb, sems, wg_bf, wu_bf, wd_bf):
    i = pl.program_id(0)
    n_used = nused_ref[0]

    def gather(tile, buf, sem, unrolled):
        base = jnp.minimum(tile, MOE_TILES - 1) * MOE_TM
        if unrolled:
            for n in range(MOE_TM):
                r = (n % GATHER_STRIDE) * (MOE_TM // GATHER_STRIDE) + n // GATHER_STRIDE
                _row_copy(h_ref, src_ref[base + r], buf, r // SUBLANES, r % SUBLANES, sem).start(priority=n % 2)
        else:
            _gather_rows(h_ref, lambda r: src_ref[base + r], buf, MOE_TM, sem)

    @pl.when(i == 0)
    def _():
        gather(0, xa, sems.at[0], False)
        gather(1, xb, sems.at[1], False)

    @pl.when((first_ref[jnp.minimum(i, MOE_TILES - 1)] == 1) & (i < MOE_TILES))
    def _():
        wg_bf[...] = wg_ref[0, 0].astype(BF16)
        wu_bf[...] = wu_ref[0, 0].astype(BF16)
        wd_bf[...] = wd_ref[0, 0].astype(BF16)

    for parity, buf in enumerate((xa, xb)):
        sem = sems.at[parity]
        mine = i % 2 == parity

        @pl.when((i < n_used) & mine)
        def _():
            _wait_rows(h_ref, buf, MOE_TM, sem)
            x = buf[...].reshape(MOE_TM, D_MODEL).astype(BF16)
            gather(i + 2, buf, sem, True)
            act = jax.nn.silu(_dot(x, wg_bf[...])) * _dot(x, wu_bf[...])
            y_ref[...] = _dot(act.astype(BF16), wd_bf[...])

        @pl.when((i >= n_used) & (i < n_used + 2) & mine)
        def _():
            _wait_rows(h_ref, buf, MOE_TM, sem)

    @pl.when((i >= n_used) & (i < MOE_TILES))
    def _():
        y_ref[...] = jnp.zeros_like(y_ref)


def _moe(h2, src, expert, first, n_used, w_gate, w_up, w_down, layer):
    tile = lambda i: jnp.minimum(i, MOE_TILES - 1)
    wspec = lambda shape: pl.BlockSpec((1, 1) + shape, lambda i, sr, ex, fi, nu: (layer, ex[tile(i)], 0, 0))
    xbuf = pltpu.VMEM((MOE_TM // SUBLANES, SUBLANES, D_MODEL), F32)
    return pl.pallas_call(
        _moe_kernel,
        grid_spec=pltpu.PrefetchScalarGridSpec(
            num_scalar_prefetch=4,
            grid=(MOE_TILES + 2,),
            in_specs=[pl.BlockSpec(memory_space=pl.ANY),
                      wspec((D_MODEL, EXPERT_FF)), wspec((D_MODEL, EXPERT_FF)), wspec((EXPERT_FF, D_MODEL))],
            out_specs=pl.BlockSpec((MOE_TM, D_MODEL), lambda i, sr, ex, fi, nu: (tile(i), 0)),
            scratch_shapes=[xbuf, xbuf, pltpu.SemaphoreType.DMA((2,)),
                            pltpu.VMEM((D_MODEL, EXPERT_FF), BF16), pltpu.VMEM((D_MODEL, EXPERT_FF), BF16),
                            pltpu.VMEM((EXPERT_FF, D_MODEL), BF16)],
        ),
        out_shape=jax.ShapeDtypeStruct((MOE_TILES * MOE_TM, D_MODEL), F32),
        compiler_params=_params(("arbitrary",), disable_bounds_checks=True),
        name=f"moe{layer}",
    )(src, expert, first, n_used, h2, w_gate, w_up, w_down)


def _combine_kernel(last, pos0_ref, pos1_ref, ys_ref, x_ref, w_ref, mod_ref, g_ref, nmod_ref, *refs):
    outs, (ya, yb, sems) = refs[:2], refs[2:]
    i = pl.program_id(0)
    tm = x_ref.shape[0]
    n = T_ALL // tm

    def gather(tile, buf, sem, unrolled):
        base = jnp.minimum(tile, n - 1) * tm
        for choice, pos_ref in enumerate((pos0_ref, pos1_ref)):
            if unrolled:
                for r in range(tm):
                    _row_copy(ys_ref, pos_ref[base + r], buf.at[choice], r // SUBLANES, r % SUBLANES,
                              sem).start(priority=r % 2)
            else:
                _gather_rows(ys_ref, lambda r: pos_ref[base + r], buf.at[choice], tm, sem)

    @pl.when(i == 0)
    def _():
        gather(0, ya, sems.at[0], False)
        gather(1, yb, sems.at[1], False)

    for parity, buf in enumerate((ya, yb)):
        sem = sems.at[parity]
        mine = i % 2 == parity

        @pl.when((i < n) & mine)
        def _():
            _wait_rows(ys_ref, buf.at[0], TOP_K * tm, sem)
            w0 = _row_to_col(w_ref[0:1, :])
            w1 = _row_to_col(w_ref[1:2, :])
            y = w0 * buf[0].reshape(tm, D_MODEL) + w1 * buf[1].reshape(tm, D_MODEL)
            gather(i + 2, buf, sem, True)
            x = x_ref[...] + mod_ref[0, 0, G2:G2 + 1, :] * y
            if last:
                yc_ref, ys_out_ref = outs
                final = _rms(x) * g_ref[0]

                @pl.when(i < T_CTX // tm)
                def _():
                    yc_ref[...] = final

                @pl.when(i >= T_CTX // tm)
                def _():
                    ys_out_ref[...] = final
            else:
                xo_ref, h_ref = outs
                xo_ref[...] = x
                h_ref[...] = _mod_norm(x, g_ref[0], nmod_ref, SH1, SC1).astype(BF16)

        @pl.when((i >= n) & mine)
        def _():
            _wait_rows(ys_ref, buf.at[0], TOP_K * tm, sem)


def _combine(ys, pos, w_t, x, mods, norm_g, layer):
    tm = COMBINE_TM
    n = T_ALL // tm
    last = layer == DEPTH - 1
    nxt = 0 if last else layer + 1
    tile = lambda i: jnp.minimum(i, n - 1)
    tok = pl.BlockSpec((tm, D_MODEL), lambda i, p0, p1: (tile(i), 0))
    if last:
        out_specs = _two_group_specs(tm, D_MODEL)
        out_shape = [jax.ShapeDtypeStruct((T_CTX, D_MODEL), F32), jax.ShapeDtypeStruct((T_SMP, D_MODEL), F32)]
    else:
        out_specs = [tok, tok]
        out_shape = [jax.ShapeDtypeStruct((T_ALL, D_MODEL), F32), jax.ShapeDtypeStruct((T_ALL, D_MODEL), BF16)]
    ybuf = pltpu.VMEM((TOP_K, tm // SUBLANES, SUBLANES, D_MODEL), F32)
    return pl.pallas_call(
        functools.partial(_combine_kernel, last),
        grid_spec=pltpu.PrefetchScalarGridSpec(
            num_scalar_prefetch=2,
            grid=(n + 2,),
            in_specs=[pl.BlockSpec(memory_space=pl.ANY),
                      tok,
                      pl.BlockSpec((TOP_K, tm), lambda i, p0, p1: (0, tile(i))),
                      _mod_spec(layer, tm),
                      pl.BlockSpec((1, 1, D_MODEL), lambda i, p0, p1: (nxt, 0, 0)),
                      _mod_spec(nxt, tm)],
            out_specs=out_specs,
            scratch_shapes=[ybuf, ybuf, pltpu.SemaphoreType.DMA((2,))],
        ),
        out_shape=out_shape,
        compiler_params=_params(("arbitrary",), disable_bounds_checks=True),
        name=f"combine{layer}",
    )(pos[0], pos[1], ys, x, w_t, mods, norm_g, mods)


def _stack_kernel(*refs):
    ins, outs = refs[:2 * DEPTH], refs[2 * DEPTH:]
    for t, out_ref in enumerate(outs):
        for l in range(DEPTH):
            out_ref[0, l] = ins[t * DEPTH + l][...]


def _stack_kv(ks, vs):
    seq = pl.BlockSpec((SEQ, NA_WIDTH), lambda b: (b, 0))
    out = pl.BlockSpec((1, DEPTH, SEQ, NA_WIDTH), lambda b: (b, 0, 0, 0))
    shape = jax.ShapeDtypeStruct((BATCH, DEPTH, SEQ, NA_WIDTH), F32)
    return pl.pallas_call(
        _stack_kernel,
        grid=(BATCH,),
        in_specs=[seq] * (2 * DEPTH),
        out_specs=[out, out],
        out_shape=[shape, shape],
        compiler_params=_params(("arbitrary",)),
        name="stack_kv",
    )(*ks, *vs)


def kernel(x_prompt, x_sample, c, cache_k, cache_v, c_ctx, ada_w, ada_b, norm1_g, w_in, rpb, pool_w, pool_scale,
           w_branch_a, w_branch_p, w_out, norm2_g, router_w, router_bias, moe_w_gate, moe_w_up, moe_w_down, final_g):
    x_ctx = x_prompt.reshape(T_CTX, D_MODEL)
    x_smp = x_sample.reshape(T_SMP, D_MODEL)
    cond = jnp.concatenate([c_ctx[None, :], c, jnp.zeros((N_COND - 1 - DEC_BATCH, D_MODEL), F32)], axis=0)
    mods = _ada(cond, ada_w, ada_b).reshape(DEPTH, N_COND, N_MOD, D_MODEL)
    rpb_pad = jnp.pad(rpb.astype(F32), ((0, 0), (0, 0), (0, 1), (0, LANES - rpb.shape[-1])))
    w_ba, w_bp, w_o = w_branch_a.astype(BF16), w_branch_p.astype(BF16), w_out.astype(BF16)
    router_wt = router_w.T
    norm1 = norm1_g.reshape(DEPTH, 1, D_MODEL)

    h = _pre(x_ctx, x_smp, norm1_g, mods)
    x_parts = (x_ctx, x_smp)
    new_ks, new_vs = [], []
    for l in range(DEPTH):
        q, kc, ks, vc, vs, u, gates = _inproj(h, w_in, l)
        new_ks.append(kc)
        new_vs.append(vc)
        oa_c, op_c = _ctx_mixer(q, kc, vc, u, pool_w, pool_scale, l)
        oa_s, op_s = _na_mixer(q, ks, vs, u, cache_k, cache_v, rpb_pad, pool_w, pool_scale, l)
        mix = _branch(oa_c, op_c, oa_s, op_s, gates, w_ba, w_bp, l)
        x, h2, logits_t = _merge(mix, x_parts, mods, w_o, norm2_g, router_wt, l)
        idx_t, w_t, rank_t, counts = _route(logits_t, router_bias)
        seg_start, expert, first, n_used, pad_lo, pad_hi = _tile_plan(counts[:, 0])
        pos = _slots(idx_t, rank_t, seg_start)
        src = _invert(pos, pad_lo, pad_hi)
        ys = _moe(h2, src, expert, first, n_used, moe_w_gate, moe_w_up, moe_w_down, l)
        if l < DEPTH - 1:
            x, h = _combine(ys, pos, w_t, x, mods, norm1, l)
            x_parts = (x,)
        else:
            y_ctx, y_smp = _combine(ys, pos, w_t, x, mods, final_g.reshape(1, 1, D_MODEL), l)
    new_k, new_v = _stack_kv(new_ks, new_vs)
    heads = (BATCH, DEPTH, SEQ, NA_HEADS, NA_HEAD_DIM)
    return (y_ctx.reshape(BATCH, SEQ, D_MODEL), y_smp.reshape(DEC_BATCH, DEC_SEQ, D_MODEL),
            new_k.reshape(heads), new_v.reshape(heads))
```

```python
import functools

import numpy as np
import jax
import jax.numpy as jnp
from jax import lax
from jax.experimental import pallas as pl
from jax.experimental.pallas import tpu as pltpu

D_MODEL = 2048
BATCH = 32
SEQ = 256
DEPTH = 2
DEC_BATCH = 2
DEC_SEQ = 2048
PAST_LEN = 256
GRID_W = 64
GRID_ROWS = DEC_SEQ // GRID_W
NA_HEADS = 16
NA_HEAD_DIM = 64
NA_WIDTH = NA_HEADS * NA_HEAD_DIM
WIN_H = 8
WIN_W = 16
POOL_WIDTH = D_MODEL // 2
POOL_WINDOWS = (2, 4, 8, 16)
POOL_GROUPS = len(POOL_WINDOWS)
POOL_GC = POOL_WIDTH // POOL_GROUPS
IN_WIDTH = 3 * NA_WIDTH + POOL_WIDTH + 2 * D_MODEL
N_EXPERTS = 16
N_GROUPS = 4
EXPERTS_PER_GROUP = N_EXPERTS // N_GROUPS
TOP_K = 2
EXPERT_FF = D_MODEL // 4
N_MOD = 6
EPS = 1e-6

T_CTX = BATCH * SEQ
T_SMP = DEC_BATCH * DEC_SEQ
T_ALL = T_CTX + T_SMP
N_COND = 8
NA_KEYS = WIN_H * GRID_W
MASKED = -1e30

F32 = jnp.float32
BF16 = jnp.bfloat16
VMEM_LIMIT = 56 * 1024 * 1024

SH1, SC1, G1, SH2, SC2, G2 = range(6)
LANES = 128
PAIR = 2 * NA_HEAD_DIM
ROUTE_BLOCK = 512
MOE_TM = 256
MOE_TILES = T_ALL * TOP_K // MOE_TM + N_EXPERTS
COMBINE_TM = 256
SLAB = 16
SLAB_W = D_MODEL // SLAB


def _params(sem, vmem=VMEM_LIMIT, **kw):
    return pltpu.CompilerParams(dimension_semantics=sem, vmem_limit_bytes=vmem, **kw)


def _cond_of_tile(i, tm):
    t0 = i * tm
    return jnp.where(t0 < T_CTX, 0, 1 + (t0 - T_CTX) // DEC_SEQ)


def _mod_spec(layer, tm):
    last = T_ALL // tm - 1
    return pl.BlockSpec((1, 1, N_MOD, D_MODEL),
                        lambda i, *_: (layer, _cond_of_tile(jnp.minimum(i, last), tm), 0, 0))


def _two_group_specs(tm, width):
    nc = T_CTX // tm
    ns = T_SMP // tm
    return [pl.BlockSpec((tm, width), lambda i, *_: (jnp.minimum(i, nc - 1), 0)),
            pl.BlockSpec((tm, width), lambda i, *_: (jnp.clip(i - nc, 0, ns - 1), 0))]


def _rms(x):
    return x * lax.rsqrt(jnp.mean(x * x, axis=-1, keepdims=True) + EPS)


def _mod_norm(x, g, mod_ref, shift_row, scale_row):
    return (_rms(x) * g * (1.0 + mod_ref[0, 0, scale_row:scale_row + 1, :])
            + mod_ref[0, 0, shift_row:shift_row + 1, :])


def _dot(a, b):
    return jnp.dot(a, b, preferred_element_type=F32)


def _dot_nt(a, b):
    return lax.dot_general(a, b, (((1,), (1,)), ((), ())), preferred_element_type=F32)


def _split2(x):
    x1 = x.astype(BF16)
    return x1, (x - x1.astype(F32)).astype(BF16)


def _row_to_col(v):
    n = v.shape[1]
    r = lax.broadcasted_iota(jnp.int32, (n, n), 0)
    c = lax.broadcasted_iota(jnp.int32, (n, n), 1)
    return jnp.sum(jnp.where(r == c, v, 0.0), axis=1, keepdims=True)


def _ada_kernel(c_ref, w_ref, b_ref, o_ref):
    s = jax.nn.silu(c_ref[...]).astype(BF16)
    o_ref[0] = _dot(s, w_ref[0].astype(BF16)) + b_ref[0]


def _ada(cond, ada_w, ada_b):
    tn = 1024
    nj = N_MOD * D_MODEL // tn
    return pl.pallas_call(
        _ada_kernel,
        grid=(DEPTH, nj),
        in_specs=[
            pl.BlockSpec((N_COND, D_MODEL), lambda l, j: (0, 0)),
            pl.BlockSpec((1, D_MODEL, tn), lambda l, j: (l, 0, j)),
            pl.BlockSpec((1, 1, tn), lambda l, j: (l, 0, j)),
        ],
        out_specs=pl.BlockSpec((1, N_COND, tn), lambda l, j: (l, 0, j)),
        out_shape=jax.ShapeDtypeStruct((DEPTH, N_COND, N_MOD * D_MODEL), F32),
        compiler_params=_params(("arbitrary", "arbitrary")),
        name="ada",
    )(cond, ada_w, ada_b.reshape(DEPTH, 1, N_MOD * D_MODEL))


def _pre_kernel(xc_ref, xs_ref, g_ref, mod_ref, h_ref):
    is_ctx = pl.program_id(0) < T_CTX // h_ref.shape[0]
    x = jnp.where(is_ctx, xc_ref[...], xs_ref[...])
    h_ref[...] = _mod_norm(x, g_ref[0], mod_ref, SH1, SC1).astype(BF16)


def _pre(x_ctx, x_smp, norm_g, mods):
    tm = 512
    return pl.pallas_call(
        _pre_kernel,
        grid=(T_ALL // tm,),
        in_specs=_two_group_specs(tm, D_MODEL) + [
            pl.BlockSpec((1, 1, D_MODEL), lambda i: (0, 0, 0)),
            _mod_spec(0, tm),
        ],
        out_specs=pl.BlockSpec((tm, D_MODEL), lambda i: (i, 0)),
        out_shape=jax.ShapeDtypeStruct((T_ALL, D_MODEL), BF16),
        compiler_params=_params(("arbitrary",)),
        name="pre",
    )(x_ctx, x_smp, norm_g.reshape(DEPTH, 1, D_MODEL), mods)


INPROJ_TM = 512
INPROJ_TN = 1024
_NI = T_ALL // INPROJ_TM
_NC = T_CTX // INPROJ_TM
_COLS = {name: (start // INPROJ_TN, width // INPROJ_TN) for name, start, width in (
    ("q", 0, NA_WIDTH), ("k", NA_WIDTH, NA_WIDTH), ("v", 2 * NA_WIDTH, NA_WIDTH),
    ("u", 3 * NA_WIDTH, POOL_WIDTH), ("g", 3 * NA_WIDTH + POOL_WIDTH, 2 * D_MODEL))}


def _visit(name, rows, last_row):
    j0, nj = _COLS[name]

    def index(j, i):
        row = jnp.where(j < j0, 0, jnp.where(j < j0 + nj, rows(i), last_row))
        return row, jnp.clip(j - j0, 0, nj - 1)
    return index


def _inproj_kernel(h_ref, w_ref, q_ref, kc_ref, ks_ref, vc_ref, vs_ref, u_ref, g_ref, wbf_ref):
    j = pl.program_id(0)
    i = pl.program_id(1)

    @pl.when(i == 0)
    def _():
        wbf_ref[...] = w_ref[0].astype(BF16)

    def project():
        return _dot(h_ref[...], wbf_ref[...])

    def during(name):
        j0, nj = _COLS[name]
        return (j >= j0) & (j < j0 + nj)

    @pl.when(during("q"))
    def _():
        q_ref[...] = (project() * NA_HEAD_DIM ** -0.5).astype(BF16)

    for name, ctx_ref, smp_ref in (("k", kc_ref, ks_ref), ("v", vc_ref, vs_ref)):
        @pl.when(during(name) & (i < _NC))
        def _():
            ctx_ref[...] = project()

        @pl.when(during(name) & (i >= _NC))
        def _():
            smp_ref[...] = project().astype(BF16)

    @pl.when(during("u"))
    def _():
        u_ref[...] = project()

    @pl.when(during("g"))
    def _():
        g_ref[...] = project().astype(BF16)


def _inproj(h, w_in, layer):
    tm, tn = INPROJ_TM, INPROJ_TN
    blk = pl.BlockSpec
    every = lambda i: i
    ctx_rows = lambda i: jnp.minimum(i, _NC - 1)
    smp_rows = lambda i: jnp.clip(i - _NC, 0, _NI - _NC - 1)
    sds = jax.ShapeDtypeStruct
    return pl.pallas_call(
        _inproj_kernel,
        grid=(IN_WIDTH // tn, _NI),
        in_specs=[blk((tm, D_MODEL), lambda j, i: (i, 0)),
                  blk((1, D_MODEL, tn), lambda j, i: (layer, 0, j))],
        out_specs=[blk((tm, tn), _visit("q", every, _NI - 1)),
                   blk((tm, tn), _visit("k", ctx_rows, _NC - 1)),
                   blk((tm, tn), _visit("k", smp_rows, _NI - _NC - 1)),
                   blk((tm, tn), _visit("v", ctx_rows, _NC - 1)),
                   blk((tm, tn), _visit("v", smp_rows, _NI - _NC - 1)),
                   blk((tm, tn), _visit("u", every, _NI - 1)),
                   blk((tm, tn), _visit("g", every, _NI - 1))],
        out_shape=[sds((T_ALL, NA_WIDTH), BF16),
                   sds((T_CTX, NA_WIDTH), F32), sds((T_SMP, NA_WIDTH), BF16),
                   sds((T_CTX, NA_WIDTH), F32), sds((T_SMP, NA_WIDTH), BF16),
                   sds((T_ALL, POOL_WIDTH), F32),
                   sds((T_ALL, 2 * D_MODEL), BF16)],
        scratch_shapes=[pltpu.VMEM((D_MODEL, tn), BF16)],
        compiler_params=_params(("arbitrary", "arbitrary")),
        name=f"inproj{layer}",
    )(h, w_in)


def _pool_consts(n, seg):
    t = np.arange(n)
    pos = t % seg
    base = t - pos
    mats = np.zeros((POOL_GROUPS, n, n), np.float32)
    cnts = np.zeros((POOL_GROUPS, n, 1), np.float32)
    for gi, w in enumerate(POOL_WINDOWS):
        lo = np.clip(pos - w // 2, 0, seg)
        hi = np.clip(pos + w - w // 2, 0, seg)
        s = t[None, :]
        mats[gi] = ((s >= (base + lo)[:, None]) & (s < (base + hi)[:, None])).astype(np.float32)
        cnts[gi, :, 0] = hi - lo
    return jnp.asarray(mats, BF16), jnp.asarray(cnts, F32)


def _pool(u_ref, band_ref, cnt_ref, pw_ref, ps_ref, op_ref):
    for gi in range(POOL_GROUPS):
        cols = slice(gi * POOL_GC, (gi + 1) * POOL_GC)
        ug = u_ref[:, cols]
        u1 = ug.astype(BF16)
        u2 = (ug - u1.astype(F32)).astype(BF16)
        u3 = (ug - u1.astype(F32) - u2.astype(F32)).astype(BF16)
        band = band_ref[gi]
        wsum = _dot(band, u1) + _dot(band, u2) + _dot(band, u3)
        d = (wsum / cnt_ref[gi] - ug).astype(BF16)
        y = _dot(d, pw_ref[0, gi].astype(BF16))
        op_ref[:, cols] = (y * ps_ref[0, :, cols]).astype(BF16)


def _softmax_av(s_parts, v_parts):
    m = s_parts[0].max(axis=-1, keepdims=True)
    for s in s_parts[1:]:
        m = jnp.maximum(m, s.max(axis=-1, keepdims=True))
    den = None
    acc = None
    for s, v in zip(s_parts, v_parts):
        p = jnp.exp(s - m)
        ps = p.sum(axis=-1, keepdims=True)
        den = ps if den is None else den + ps
        pv = _dot(p.astype(BF16), v)
        acc = pv if acc is None else acc + pv
    return acc / den


def _attend_pair(q, ks, vs, biases):
    m = q.shape[0]
    first = lax.broadcasted_iota(jnp.int32, (m, PAIR), 1) < NA_HEAD_DIM
    qf = q.astype(F32)
    q2 = jnp.concatenate([jnp.where(first, qf, 0.0), jnp.where(first, 0.0, qf)], axis=0).astype(BF16)
    scores = [_dot_nt(q2, k) if b is None else _dot_nt(q2, k) + b for k, b in zip(ks, biases)]
    o2 = _softmax_av(scores, vs)
    return jnp.where(first, o2[:m], o2[m:])


def _ctx_mixer_kernel(q_ref, k_ref, v_ref, u_ref, band_ref, cnt_ref, pw_ref, ps_ref, oa_ref, op_ref):
    for hp in range(NA_HEADS // 2):
        cols = slice(hp * PAIR, (hp + 1) * PAIR)
        o = _attend_pair(q_ref[:, cols], [k_ref[:, cols].astype(BF16)], [v_ref[:, cols].astype(BF16)], [None])
        oa_ref[:, cols] = o.astype(BF16)
    _pool(u_ref, band_ref, cnt_ref, pw_ref, ps_ref, op_ref)


def _ctx_mixer(q, kc, vc, u, pool_w, pool_scale, layer):
    band, cnt = _pool_consts(SEQ, SEQ)
    seq = lambda width: pl.BlockSpec((SEQ, width), lambda s: (s, 0))
    return pl.pallas_call(
        _ctx_mixer_kernel,
        grid=(BATCH,),
        in_specs=[
            seq(NA_WIDTH), seq(NA_WIDTH), seq(NA_WIDTH), seq(POOL_WIDTH),
            pl.BlockSpec((POOL_GROUPS, SEQ, SEQ), lambda s: (0, 0, 0)),
            pl.BlockSpec((POOL_GROUPS, SEQ, 1), lambda s: (0, 0, 0)),
            pl.BlockSpec((1, POOL_GROUPS, POOL_GC, POOL_GC), lambda s: (layer, 0, 0, 0)),
            pl.BlockSpec((1, 1, POOL_WIDTH), lambda s: (layer, 0, 0)),
        ],
        out_specs=[seq(NA_WIDTH), seq(POOL_WIDTH)],
        out_shape=[jax.ShapeDtypeStruct((T_CTX, NA_WIDTH), BF16),
                   jax.ShapeDtypeStruct((T_CTX, POOL_WIDTH), BF16)],
        compiler_params=_params(("arbitrary",)),
        name=f"ctx_mixer{layer}",
    )(q, kc, vc, u, band, cnt, pool_w, pool_scale.reshape(DEPTH, 1, POOL_WIDTH))


def _na_build_bias(rpb_ref, bias_ref, r, rs):
    lanes = 2 * GRID_W
    qc = lax.broadcasted_iota(jnp.int32, (GRID_W, lanes), 0)
    lane = lax.broadcasted_iota(jnp.int32, (GRID_W, lanes), 1)
    kc = jnp.bitwise_and(lane, GRID_W - 1)
    cstart = jnp.clip(qc - WIN_W // 2, 0, GRID_W - WIN_W)
    valid = (kc >= cstart) & (kc < cstart + WIN_W)
    for h in range(NA_HEADS):
        for jp in range(WIN_H // 2):
            halves = []
            for j, shift in ((2 * jp, lanes - (WIN_W - 1)), (2 * jp + 1, GRID_W - (WIN_W - 1))):
                dr = rs + j - r + (WIN_H - 1)
                row = jnp.broadcast_to(rpb_ref[0, h, pl.ds(dr, 1), :], (GRID_W, lanes))
                halves.append(pltpu.roll(row, shift, 1, stride=1, stride_axis=0))
            tile = jnp.where(lane < GRID_W, halves[0], halves[1])
            bias_ref[h, :, jp * lanes:(jp + 1) * lanes] = jnp.where(valid, tile, MASKED)


def _na_mixer_kernel(q_ref, k_ref, v_ref, ck_ref, cv_ref, rpb_ref, u_ref, band_ref, cnt_ref, pw_ref, ps_ref,
                     oa_ref, op_ref, bias_ref, ckb_ref, cvb_ref):
    r = pl.program_id(1)
    rs = jnp.clip(r - WIN_H // 2, 0, GRID_ROWS - WIN_H)

    @pl.when(r == 0)
    def _():
        ckb_ref[...] = ck_ref[0, 0].astype(BF16)
        cvb_ref[...] = cv_ref[0, 0].astype(BF16)

    @pl.when((r <= WIN_H // 2) | (r > GRID_ROWS - WIN_H // 2))
    def _():
        _na_build_bias(rpb_ref, bias_ref, r, rs)

    keys = pl.ds(pl.multiple_of(rs * GRID_W, GRID_W), NA_KEYS)
    for hp in range(NA_HEADS // 2):
        cols = slice(hp * PAIR, (hp + 1) * PAIR)
        bias = bias_ref[2 * hp:2 * hp + 2].reshape(2 * GRID_W, NA_KEYS)
        o = _attend_pair(q_ref[:, cols], [k_ref[keys, cols], ckb_ref[:, cols]],
                         [v_ref[keys, cols], cvb_ref[:, cols]], [bias, None])
        oa_ref[:, cols] = o.astype(BF16)
    _pool(u_ref, band_ref, cnt_ref, pw_ref, ps_ref, op_ref)


def _na_mixer(q, ks, vs, u, cache_k, cache_v, rpb_pad, pool_w, pool_scale, layer):
    band, cnt = _pool_consts(GRID_W, GRID_W)
    row0 = T_CTX // GRID_W
    row = lambda width: pl.BlockSpec((GRID_W, width), lambda b, r: (row0 + b * GRID_ROWS + r, 0))
    batch = pl.BlockSpec((DEC_SEQ, NA_WIDTH), lambda b, r: (b, 0))
    cblk = pl.BlockSpec((1, 1, PAST_LEN, NA_WIDTH), lambda b, r: (b, layer, 0, 0))
    out = lambda width: pl.BlockSpec((GRID_W, width), lambda b, r: (b * GRID_ROWS + r, 0))
    return pl.pallas_call(
        _na_mixer_kernel,
        grid=(DEC_BATCH, GRID_ROWS),
        in_specs=[
            row(NA_WIDTH), batch, batch, cblk, cblk,
            pl.BlockSpec((1,) + rpb_pad.shape[1:], lambda b, r: (layer, 0, 0, 0)),
            row(POOL_WIDTH),
            pl.BlockSpec((POOL_GROUPS, GRID_W, GRID_W), lambda b, r: (0, 0, 0)),
            pl.BlockSpec((POOL_GROUPS, GRID_W, 1), lambda b, r: (0, 0, 0)),
            pl.BlockSpec((1, POOL_GROUPS, POOL_GC, POOL_GC), lambda b, r: (layer, 0, 0, 0)),
            pl.BlockSpec((1, 1, POOL_WIDTH), lambda b, r: (layer, 0, 0)),
        ],
        out_specs=[out(NA_WIDTH), out(POOL_WIDTH)],
        out_shape=[jax.ShapeDtypeStruct((T_SMP, NA_WIDTH), BF16),
                   jax.ShapeDtypeStruct((T_SMP, POOL_WIDTH), BF16)],
        scratch_shapes=[pltpu.VMEM((NA_HEADS, GRID_W, NA_KEYS), F32),
                        pltpu.VMEM((PAST_LEN, NA_WIDTH), BF16), pltpu.VMEM((PAST_LEN, NA_WIDTH), BF16)],
        compiler_params=_params(("arbitrary", "arbitrary")),
        name=f"na_mixer{layer}",
    )(q, ks, vs,
      cache_k.reshape(DEC_BATCH, DEPTH, PAST_LEN, NA_WIDTH), cache_v.reshape(DEC_BATCH, DEPTH, PAST_LEN, NA_WIDTH),
      rpb_pad, u, band, cnt, pool_w, pool_scale.reshape(DEPTH, 1, POOL_WIDTH))


MERGE_TM = 512
MERGE_HALF = MERGE_TM // 2


def _halves():
    return [slice(k * MERGE_HALF, (k + 1) * MERGE_HALF) for k in range(2)]


def _branch_kernel(oac_ref, opc_ref, oas_ref, ops_ref, ga_ref, gp_ref, wba_ref, wbp_ref, mix_ref):
    is_ctx = pl.program_id(0) < T_CTX // MERGE_TM
    for rows in _halves():
        a = _dot(jnp.where(is_ctx, oac_ref[rows, :], oas_ref[rows, :]), wba_ref[0])
        p = _dot(jnp.where(is_ctx, opc_ref[rows, :], ops_ref[rows, :]), wbp_ref[0])
        mix = (jax.nn.sigmoid(ga_ref[rows, :].astype(F32)) * a
               + jax.nn.sigmoid(gp_ref[rows, :].astype(F32)) * p)
        mix_ref[rows, :] = mix.astype(BF16)


def _branch(oa_c, op_c, oa_s, op_s, gates, w_ba, w_bp, layer):
    tm = MERGE_TM
    full = lambda shape: pl.BlockSpec(shape, lambda i: (layer,) + (0,) * (len(shape) - 1))
    na, pool = _two_group_specs(tm, NA_WIDTH), _two_group_specs(tm, POOL_WIDTH)
    return pl.pallas_call(
        _branch_kernel,
        grid=(T_ALL // tm,),
        in_specs=[na[0], pool[0], na[1], pool[1],
                  pl.BlockSpec((tm, D_MODEL), lambda i: (i, 0)),
                  pl.BlockSpec((tm, D_MODEL), lambda i: (i, 1)),
                  full((1, NA_WIDTH, D_MODEL)),
                  full((1, POOL_WIDTH, D_MODEL))],
        out_specs=pl.BlockSpec((tm, D_MODEL), lambda i: (i, 0)),
        out_shape=jax.ShapeDtypeStruct((T_ALL, D_MODEL), BF16),
        compiler_params=_params(("arbitrary",)),
        name=f"branch{layer}",
    )(oa_c, op_c, oa_s, op_s, gates, gates, w_ba, w_bp)


def _merge_kernel(n_x, mix_ref, *refs):
    x_refs, (mod_ref, wout_ref, n2g_ref, rwt_ref, xo_ref, h2_ref, lg_ref) = refs[:n_x], refs[n_x:]
    is_ctx = pl.program_id(0) < T_CTX // MERGE_TM
    w1, w2 = _split2(rwt_ref[...])
    for rows in _halves():
        m = _dot(mix_ref[rows, :], wout_ref[0])
        x_in = x_refs[0][rows, :] if n_x == 1 else jnp.where(is_ctx, x_refs[0][rows, :], x_refs[1][rows, :])
        x = x_in + mod_ref[0, 0, G1:G1 + 1, :] * m
        xo_ref[rows, :] = x
        h2 = _mod_norm(x, n2g_ref[0], mod_ref, SH2, SC2)
        for s in range(SLAB):
            h2_ref[pl.ds(rows.start * SLAB + s, MERGE_HALF, stride=SLAB), :] = h2[:, s * SLAB_W:(s + 1) * SLAB_W]
        t1, t2 = _split2(h2)
        lg_ref[:, rows] = _dot_nt(w1, t1) + _dot_nt(w1, t2) + _dot_nt(w2, t1)


def _merge(mix, x_parts, mods, w_out, norm2_g, router_wt, layer):
    tm = MERGE_TM
    full = lambda shape: pl.BlockSpec(shape, lambda i: (layer,) + (0,) * (len(shape) - 1))
    tok = pl.BlockSpec((tm, D_MODEL), lambda i: (i, 0))
    x_specs = [tok] if len(x_parts) == 1 else _two_group_specs(tm, D_MODEL)
    return pl.pallas_call(
        functools.partial(_merge_kernel, len(x_parts)),
        grid=(T_ALL // tm,),
        in_specs=[tok] + x_specs + [
            _mod_spec(layer, tm),
            full((1, D_MODEL, D_MODEL)),
            full((1, 1, D_MODEL)),
            pl.BlockSpec((N_EXPERTS, D_MODEL), lambda i: (0, 0)),
        ],
        out_specs=[tok, pl.BlockSpec((tm * SLAB, SLAB_W), lambda i: (i, 0)),
                   pl.BlockSpec((N_EXPERTS, tm), lambda i: (0, i))],
        out_shape=[jax.ShapeDtypeStruct((T_ALL, D_MODEL), F32),
                   jax.ShapeDtypeStruct((T_ALL * SLAB, SLAB_W), F32),
                   jax.ShapeDtypeStruct((N_EXPERTS, T_ALL), F32)],
        compiler_params=_params(("arbitrary",)),
        name=f"merge{layer}",
    )(mix, *x_parts, mods, w_out, norm2_g.reshape(DEPTH, 1, D_MODEL), router_wt)


def _route_kernel(lg_ref, rb_ref, tri_ref, idx_ref, w_ref, rank_ref, cnt_ref, carry_ref):
    @pl.when(pl.program_id(0) == 0)
    def _():
        carry_ref[...] = jnp.zeros_like(carry_ref)

    scores = jax.nn.sigmoid(lg_ref[...])
    sel = scores + rb_ref[...]
    gscore = []
    for g in range(N_GROUPS):
        a, b, c, d = [sel[EXPERTS_PER_GROUP * g + i:EXPERTS_PER_GROUP * g + i + 1, :] for i in range(4)]
        hi1, lo1, hi2, lo2 = jnp.maximum(a, b), jnp.minimum(a, b), jnp.maximum(c, d), jnp.minimum(c, d)
        gscore.append(jnp.maximum(hi1, hi2) + jnp.maximum(jnp.minimum(hi1, hi2), jnp.maximum(lo1, lo2)))
    best = jnp.zeros_like(gscore[0], dtype=jnp.int32)
    bestv = gscore[0]
    for g in range(1, N_GROUPS):
        upd = gscore[g] > bestv
        best = jnp.where(upd, g, best)
        bestv = jnp.where(upd, gscore[g], bestv)
    row = lax.broadcasted_iota(jnp.int32, sel.shape, 0)
    masked = jnp.where(row // EXPERTS_PER_GROUP == best, sel, -jnp.inf)
    i0 = jnp.min(jnp.where(masked == masked.max(axis=0, keepdims=True), row, N_EXPERTS), axis=0, keepdims=True)
    masked = jnp.where(row == i0, -jnp.inf, masked)
    i1 = jnp.min(jnp.where(masked == masked.max(axis=0, keepdims=True), row, N_EXPERTS), axis=0, keepdims=True)
    s0 = jnp.sum(jnp.where(row == i0, scores, 0.0), axis=0, keepdims=True)
    s1 = jnp.sum(jnp.where(row == i1, scores, 0.0), axis=0, keepdims=True)
    den = s0 + s1
    idx_ref[...] = jnp.concatenate([i0, i1], axis=0)
    w_ref[...] = jnp.concatenate([s0 / den, s1 / den], axis=0)

    hit = jnp.where((row == i0) | (row == i1), 1.0, 0.0)
    before = _dot(hit.astype(BF16), tri_ref[...]) + carry_ref[:, 0:1]
    r0 = jnp.sum(jnp.where(row == i0, before, 0.0), axis=0, keepdims=True)
    r1 = jnp.sum(jnp.where(row == i1, before, 0.0), axis=0, keepdims=True)
    rank_ref[...] = jnp.concatenate([r0, r1], axis=0).astype(jnp.int32)
    carry_ref[...] = carry_ref[...] + jnp.sum(hit, axis=1, keepdims=True)
    cnt_ref[...] = carry_ref[...]


def _route(logits_t, router_bias):
    tn = ROUTE_BLOCK
    tri = jnp.asarray(np.triu(np.ones((tn, tn), np.float32), k=1), BF16)
    pair = lambda dt: jax.ShapeDtypeStruct((TOP_K, T_ALL), dt)
    return pl.pallas_call(
        _route_kernel,
        grid=(T_ALL // tn,),
        in_specs=[pl.BlockSpec((N_EXPERTS, tn), lambda i: (0, i)),
                  pl.BlockSpec((N_EXPERTS, 1), lambda i: (0, 0)),
                  pl.BlockSpec((tn, tn), lambda i: (0, 0))],
        out_specs=[pl.BlockSpec((TOP_K, tn), lambda i: (0, i)),
                   pl.BlockSpec((TOP_K, tn), lambda i: (0, i)),
                   pl.BlockSpec((TOP_K, tn), lambda i: (0, i)),
                   pl.BlockSpec((N_EXPERTS, LANES), lambda i: (0, 0))],
        out_shape=[pair(jnp.int32), pair(F32), pair(jnp.int32),
                   jax.ShapeDtypeStruct((N_EXPERTS, LANES), F32)],
        scratch_shapes=[pltpu.VMEM((N_EXPERTS, LANES), F32)],
        compiler_params=_params(("arbitrary",)),
        name="route",
    )(logits_t, router_bias.reshape(N_EXPERTS, 1).astype(F32), tri)


def _slot_kernel(idx_ref, rank_ref, off_ref, pos_ref):
    row = lax.broadcasted_iota(jnp.int32, (N_EXPERTS, idx_ref.shape[1]), 0)
    for j in range(TOP_K):
        off = jnp.sum(jnp.where(row == idx_ref[j:j + 1, :], off_ref[...], 0), axis=0, keepdims=True)
        pos_ref[j:j + 1, :] = off + rank_ref[j:j + 1, :]


def _slots(idx_t, rank_t, seg_start):
    tn = ROUTE_BLOCK
    blk = pl.BlockSpec((TOP_K, tn), lambda i: (0, i))
    return pl.pallas_call(
        _slot_kernel,
        grid=(T_ALL // tn,),
        in_specs=[blk, blk, pl.BlockSpec((N_EXPERTS, 1), lambda i: (0, 0))],
        out_specs=blk,
        out_shape=jax.ShapeDtypeStruct((TOP_K, T_ALL), jnp.int32),
        compiler_params=_params(("arbitrary",)),
        name="slots",
    )(idx_t, rank_t, seg_start.reshape(N_EXPERTS, 1))


def _tile_plan(counts):
    cnt = counts.astype(jnp.int32)
    ntile = (cnt + MOE_TM - 1) // MOE_TM
    tile_end = jnp.cumsum(ntile)
    seg_start = (tile_end - ntile) * MOE_TM
    n_used = tile_end[-1]
    tiles = jnp.arange(MOE_TILES, dtype=jnp.int32)
    live = jnp.minimum(tiles, n_used - 1)
    expert = jnp.sum((live[:, None] >= tile_end[None, :]).astype(jnp.int32), axis=1)
    first = ((live == (tile_end - ntile)[expert]) & (tiles < n_used)).astype(jnp.int32)
    pad_lo = jnp.concatenate([seg_start + cnt, (n_used * MOE_TM).reshape(1)])
    pad_hi = jnp.concatenate([tile_end * MOE_TM, jnp.full((1,), MOE_TILES * MOE_TM, jnp.int32)])
    return seg_start, expert, first, n_used.reshape(1), pad_lo, pad_hi


SUBLANES = 8


def _row_copy(src, i, dst, tile, sub, sem):
    return pltpu.make_async_copy(src.at[pl.ds(i, 1), :], dst.at[tile, pl.ds(sub, 1), :], sem)


def _gather_rows(src, index_of, dst, n_rows, sem):
    def body(tile, carry):
        for sub in range(SUBLANES):
            _row_copy(src, index_of(tile * SUBLANES + sub), dst, tile, sub, sem).start()
        return carry
    lax.fori_loop(0, n_rows // SUBLANES, body, 0)


def _wait_rows(src, dst, n_rows, sem):
    for _ in range(n_rows):
        _row_copy(src, 0, dst, 0, 0, sem).wait()


def _invert_kernel(pos0_ref, pos1_ref, pad_lo_ref, pad_hi_ref, src_ref):
    def fill(s, carry):
        src_ref[s] = 0
        return carry

    def put(t, carry):
        src_ref[pos0_ref[t]] = t
        src_ref[pos1_ref[t]] = t
        return carry

    for k in range(N_EXPERTS + 1):
        lax.fori_loop(pad_lo_ref[k], pad_hi_ref[k], fill, 0)
    lax.fori_loop(0, T_ALL, put, 0, unroll=16)


def _invert(pos, pad_lo, pad_hi):
    return pl.pallas_call(
        _invert_kernel,
        grid_spec=pltpu.PrefetchScalarGridSpec(
            num_scalar_prefetch=4,
            grid=(1,),
            in_specs=[],
            out_specs=pl.BlockSpec(memory_space=pltpu.SMEM),
        ),
        out_shape=jax.ShapeDtypeStruct((MOE_TILES * MOE_TM,), jnp.int32),
        compiler_params=_params(("arbitrary",)),
        name="invert",
    )(pos[0], pos[1], pad_lo, pad_hi)


def _moe_kernel(src_ref, expert_ref, first_ref, nused_ref, h_ref, wg_ref, wu_ref, wd_ref, y_ref,
                xa, xb, sems, wg_bf, wu_bf, wd_bf):
    i = pl.program_id(0)
    n_used = nused_ref[0]

    def slab_copy(token, r, buf, sem):
        src = h_ref.at[pl.ds(pl.multiple_of(token * SLAB, SLAB), SLAB), :]
        return pltpu.make_async_copy(src, buf.at[pl.ds(r * SLAB, SLAB), :], sem)

    def gather(tile, buf, sem):
        base = jnp.minimum(tile, MOE_TILES - 1) * MOE_TM
        for r in range(MOE_TM):
            slab_copy(src_ref[base + r], r, buf, sem).start(priority=r % 2)

    def wait(buf, sem):
        for r in range(MOE_TM):
            slab_copy(0, r, buf, sem).wait()

    @pl.when(i == 0)
    def _():
        gather(0, xa, sems.at[0])
        gather(1, xb, sems.at[1])

    @pl.when((first_ref[jnp.minimum(i, MOE_TILES - 1)] == 1) & (i < MOE_TILES))
    def _():
        wg_bf[...] = wg_ref[0, 0].astype(BF16)
        wu_bf[...] = wu_ref[0, 0].astype(BF16)
        wd_bf[...] = wd_ref[0, 0].astype(BF16)

    for parity, buf in enumerate((xa, xb)):
        sem = sems.at[parity]
        mine = i % 2 == parity

        @pl.when((i < n_used) & mine)
        def _():
            wait(buf, sem)
            x = jnp.concatenate([buf[pl.ds(s, MOE_TM, stride=SLAB), :] for s in range(SLAB)], axis=1).astype(BF16)
            gather(i + 2, buf, sem)
            act = jax.nn.silu(_dot(x, wg_bf[...])) * _dot(x, wu_bf[...])
            y_ref[...] = _dot(act.astype(BF16), wd_bf[...])

        @pl.when((i >= n_used) & (i < n_used + 2) & mine)
        def _():
            wait(buf, sem)

    @pl.when((i >= n_used) & (i < MOE_TILES))
    def _():
        y_ref[...] = jnp.zeros_like(y_ref)


def _moe(h2, src, expert, first, n_used, w_gate, w_up, w_down, layer):
    tile = lambda i: jnp.minimum(i, MOE_TILES - 1)
    wspec = lambda shape: pl.BlockSpec((1, 1) + shape, lambda i, sr, ex, fi, nu: (layer, ex[tile(i)], 0, 0))
    xbuf = pltpu.VMEM((MOE_TM * SLAB, SLAB_W), F32)
    return pl.pallas_call(
        _moe_kernel,
        grid_spec=pltpu.PrefetchScalarGridSpec(
            num_scalar_prefetch=4,
            grid=(MOE_TILES + 2,),
            in_specs=[pl.BlockSpec(memory_space=pl.ANY),
                      wspec((D_MODEL, EXPERT_FF)), wspec((D_MODEL, EXPERT_FF)), wspec((EXPERT_FF, D_MODEL))],
            out_specs=pl.BlockSpec((MOE_TM, D_MODEL), lambda i, sr, ex, fi, nu: (tile(i), 0)),
            scratch_shapes=[xbuf, xbuf, pltpu.SemaphoreType.DMA((2,)),
                            pltpu.VMEM((D_MODEL, EXPERT_FF), BF16), pltpu.VMEM((D_MODEL, EXPERT_FF), BF16),
                            pltpu.VMEM((EXPERT_FF, D_MODEL), BF16)],
        ),
        out_shape=jax.ShapeDtypeStruct((MOE_TILES * MOE_TM, D_MODEL), F32),
        compiler_params=_params(("arbitrary",), disable_bounds_checks=True),
        name=f"moe{layer}",
    )(src, expert, first, n_used, h2, w_gate, w_up, w_down)


def _combine_kernel(last, pos0_ref, pos1_ref, ys_ref, x_ref, w_ref, mod_ref, g_ref, nmod_ref, *refs):
    outs, (ya, yb, sems) = refs[:2], refs[2:]
    i = pl.program_id(0)
    tm = x_ref.shape[0]
    n = T_ALL // tm

    def gather(tile, buf, sem, unrolled):
        base = jnp.minimum(tile, n - 1) * tm
        for choice, pos_ref in enumerate((pos0_ref, pos1_ref)):
            if unrolled:
                for r in range(tm):
                    _row_copy(ys_ref, pos_ref[base + r], buf.at[choice], r // SUBLANES, r % SUBLANES,
                              sem).start(priority=r % 2)
            else:
                _gather_rows(ys_ref, lambda r: pos_ref[base + r], buf.at[choice], tm, sem)

    @pl.when(i == 0)
    def _():
        gather(0, ya, sems.at[0], False)
        gather(1, yb, sems.at[1], False)

    for parity, buf in enumerate((ya, yb)):
        sem = sems.at[parity]
        mine = i % 2 == parity

        @pl.when((i < n) & mine)
        def _():
            _wait_rows(ys_ref, buf.at[0], TOP_K * tm, sem)
            w0 = _row_to_col(w_ref[0:1, :])
            w1 = _row_to_col(w_ref[1:2, :])
            y = w0 * buf[0].reshape(tm, D_MODEL) + w1 * buf[1].reshape(tm, D_MODEL)
            gather(i + 2, buf, sem, True)
            x = x_ref[...] + mod_ref[0, 0, G2:G2 + 1, :] * y
            if last:
                yc_ref, ys_out_ref = outs
                final = _rms(x) * g_ref[0]

                @pl.when(i < T_CTX // tm)
                def _():
                    yc_ref[...] = final

                @pl.when(i >= T_CTX // tm)
                def _():
                    ys_out_ref[...] = final
            else:
                xo_ref, h_ref = outs
                xo_ref[...] = x
                h_ref[...] = _mod_norm(x, g_ref[0], nmod_ref, SH1, SC1).astype(BF16)

        @pl.when((i >= n) & mine)
        def _():
            _wait_rows(ys_ref, buf.at[0], TOP_K * tm, sem)


def _combine(ys, pos, w_t, x, mods, norm_g, layer):
    tm = COMBINE_TM
    n = T_ALL // tm
    last = layer == DEPTH - 1
    nxt = 0 if last else layer + 1
    tile = lambda i: jnp.minimum(i, n - 1)
    tok = pl.BlockSpec((tm, D_MODEL), lambda i, p0, p1: (tile(i), 0))
    if last:
        out_specs = _two_group_specs(tm, D_MODEL)
        out_shape = [jax.ShapeDtypeStruct((T_CTX, D_MODEL), F32), jax.ShapeDtypeStruct((T_SMP, D_MODEL), F32)]
    else:
        out_specs = [tok, tok]
        out_shape = [jax.ShapeDtypeStruct((T_ALL, D_MODEL), F32), jax.ShapeDtypeStruct((T_ALL, D_MODEL), BF16)]
    ybuf = pltpu.VMEM((TOP_K, tm // SUBLANES, SUBLANES, D_MODEL), F32)
    return pl.pallas_call(
        functools.partial(_combine_kernel, last),
        grid_spec=pltpu.PrefetchScalarGridSpec(
            num_scalar_prefetch=2,
            grid=(n + 2,),
            in_specs=[pl.BlockSpec(memory_space=pl.ANY),
                      tok,
                      pl.BlockSpec((TOP_K, tm), lambda i, p0, p1: (0, tile(i))),
                      _mod_spec(layer, tm),
                      pl.BlockSpec((1, 1, D_MODEL), lambda i, p0, p1: (nxt, 0, 0)),
                      _mod_spec(nxt, tm)],
            out_specs=out_specs,
            scratch_shapes=[ybuf, ybuf, pltpu.SemaphoreType.DMA((2,))],
        ),
        out_shape=out_shape,
        compiler_params=_params(("arbitrary",), disable_bounds_checks=True),
        name=f"combine{layer}",
    )(pos[0], pos[1], ys, x, w_t, mods, norm_g, mods)


def _stack_kernel(*refs):
    ins, outs = refs[:2 * DEPTH], refs[2 * DEPTH:]
    for t, out_ref in enumerate(outs):
        for l in range(DEPTH):
            out_ref[0, l] = ins[t * DEPTH + l][...]


def _stack_kv(ks, vs):
    seq = pl.BlockSpec((SEQ, NA_WIDTH), lambda b: (b, 0))
    out = pl.BlockSpec((1, DEPTH, SEQ, NA_WIDTH), lambda b: (b, 0, 0, 0))
    shape = jax.ShapeDtypeStruct((BATCH, DEPTH, SEQ, NA_WIDTH), F32)
    return pl.pallas_call(
        _stack_kernel,
        grid=(BATCH,),
        in_specs=[seq] * (2 * DEPTH),
        out_specs=[out, out],
        out_shape=[shape, shape],
        compiler_params=_params(("arbitrary",)),
        name="stack_kv",
    )(*ks, *vs)


def kernel(x_prompt, x_sample, c, cache_k, cache_v, c_ctx, ada_w, ada_b, norm1_g, w_in, rpb, pool_w, pool_scale,
           w_branch_a, w_branch_p, w_out, norm2_g, router_w, router_bias, moe_w_gate, moe_w_up, moe_w_down, final_g):
    x_ctx = x_prompt.reshape(T_CTX, D_MODEL)
    x_smp = x_sample.reshape(T_SMP, D_MODEL)
    cond = jnp.concatenate([c_ctx[None, :], c, jnp.zeros((N_COND - 1 - DEC_BATCH, D_MODEL), F32)], axis=0)
    mods = _ada(cond, ada_w, ada_b).reshape(DEPTH, N_COND, N_MOD, D_MODEL)
    rpb_pad = jnp.pad(rpb.astype(F32), ((0, 0), (0, 0), (0, 1), (0, LANES - rpb.shape[-1])))
    w_ba, w_bp, w_o = w_branch_a.astype(BF16), w_branch_p.astype(BF16), w_out.astype(BF16)
    router_wt = router_w.T
    norm1 = norm1_g.reshape(DEPTH, 1, D_MODEL)

    h = _pre(x_ctx, x_smp, norm1_g, mods)
    x_parts = (x_ctx, x_smp)
    new_ks, new_vs = [], []
    for l in range(DEPTH):
        q, kc, ks, vc, vs, u, gates = _inproj(h, w_in, l)
        new_ks.append(kc)
        new_vs.append(vc)
        oa_c, op_c = _ctx_mixer(q, kc, vc, u, pool_w, pool_scale, l)
        oa_s, op_s = _na_mixer(q, ks, vs, u, cache_k, cache_v, rpb_pad, pool_w, pool_scale, l)
        mix = _branch(oa_c, op_c, oa_s, op_s, gates, w_ba, w_bp, l)
        x, h2, logits_t = _merge(mix, x_parts, mods, w_o, norm2_g, router_wt, l)
        idx_t, w_t, rank_t, counts = _route(logits_t, router_bias)
        seg_start, expert, first, n_used, pad_lo, pad_hi = _tile_plan(counts[:, 0])
        pos = _slots(idx_t, rank_t, seg_start)
        src = _invert(pos, pad_lo, pad_hi)
        ys = _moe(h2, src, expert, first, n_used, moe_w_gate, moe_w_up, moe_w_down, l)
        if l < DEPTH - 1:
            x, h = _combine(ys, pos, w_t, x, mods, norm1, l)
            x_parts = (x,)
        else:
            y_ctx, y_smp = _combine(ys, pos, w_t, x, mods, final_g.reshape(1, 1, D_MODEL), l)
    new_k, new_v = _stack_kv(new_ks, new_vs)
    heads = (BATCH, DEPTH, SEQ, NA_HEADS, NA_HEAD_DIM)
    return (y_ctx.reshape(BATCH, SEQ, D_MODEL), y_smp.reshape(DEC_BATCH, DEC_SEQ, D_MODEL),
            new_k.reshape(heads), new_v.reshape(heads))
```
